```python
import math
import jax, jax.numpy as jnp
from jax import lax
import numpy as np


D_MODEL = 2048
BATCH = 4
SEQ = 4096
DEPTH = 1

MIX_WIDTH = D_MODEL
ATT_HEADS = 8
ATT_HEAD_DIM = (MIX_WIDTH // 2) // ATT_HEADS
ATT_WIDTH = ATT_HEADS * ATT_HEAD_DIM
DILATED_CONFIGS = ((128, 1), (512, 4), (2048, 16))
RET_HEADS = 8
RET_V_DIM = (MIX_WIDTH - ATT_WIDTH) // RET_HEADS
RET_K_DIM = RET_V_DIM // 2
RET_QK_WIDTH = RET_HEADS * RET_K_DIM
RET_WIDTH = RET_HEADS * RET_V_DIM
RET_CHUNK = 128
IN_WIDTHS = (ATT_WIDTH, ATT_WIDTH, ATT_WIDTH, RET_QK_WIDTH, RET_QK_WIDTH, RET_WIDTH, RET_WIDTH)
IN_WIDTH = sum(IN_WIDTHS)
IN_SPLITS = tuple(int(i) for i in np.cumsum(IN_WIDTHS)[:-1])
D_FF = ((8 * D_MODEL // 3 + 255) // 256) * 256
N_ADA = 9
EPS = 1e-6

kernel_name = 'hybrid_dilated_attn_retention_macaron_adaln'


def _rmsnorm(x, g):
    xf = x.astype(jnp.float32)
    y = xf * lax.rsqrt(jnp.mean(xf * xf, axis=-1, keepdims=True) + EPS)
    return (y * g.astype(jnp.float32)).astype(x.dtype)


def _modulate(h, shift, scale):
    return h * (1.0 + scale) + shift


def _swiglu(h, w_gate, w_up, w_down):
    return (jax.nn.silu(h @ w_gate) * (h @ w_up)) @ w_down


def _dilated_branch(q, k, v, window, dilation):
    b, s, h, dh = q.shape
    blk = window // dilation
    span = blk * dilation
    s_pad = -(-s // span) * span
    nb = s_pad // span
    pad = ((0, 0), (0, s_pad - s), (0, 0), (0, 0))

    def split(t):
        return jnp.pad(t, pad).reshape(b, nb, blk, dilation, h, dh)

    def with_prev(t):
        prev = jnp.pad(t, ((0, 0), (1, 0), (0, 0), (0, 0), (0, 0), (0, 0)))[:, :-1]
        return jnp.concatenate([prev, t], axis=2)

    qb = split(q)
    kk = with_prev(split(k))
    vv = with_prev(split(v))
    scores = jnp.einsum('bnqrhd,bnkrhd->bnrhqk', qb, kk).astype(jnp.float32) * (dh ** -0.5)
    qi = jnp.arange(blk)[:, None]
    kj = jnp.arange(2 * blk)[None, :]
    dist = qi + blk - kj
    band = (dist >= 0) & (dist <= blk)
    valid = band[None] & ((jnp.arange(nb)[:, None, None] > 0) | (kj >= blk)[None])
    scores = jnp.where(valid[:, None, None], scores, -jnp.inf)
    m = jnp.max(scores, axis=-1, keepdims=True)
    p = jnp.exp(scores - m)
    den = jnp.sum(p, axis=-1)
    o = jnp.einsum('bnrhqk,bnkrhd->bnqrhd', p, vv.astype(jnp.float32))
    o = o / jnp.transpose(den, (0, 1, 4, 2, 3))[..., None]
    lse = jnp.transpose(m[..., 0] + jnp.log(den), (0, 1, 4, 2, 3))
    o = o.reshape(b, s_pad, h, dh)[:, :s]
    lse = lse.reshape(b, s_pad, h)[:, :s]
    return o, lse


def _dilated_attention(q, k, v):
    outs, lses = [], []
    for window, dilation in DILATED_CONFIGS:
        o, lse = _dilated_branch(q, k, v, window, dilation)
        outs.append(o)
        lses.append(lse)
    w = jax.nn.softmax(jnp.stack(lses, axis=0), axis=0)
    return jnp.sum(w[..., None] * jnp.stack(outs, axis=0), axis=0)


def _rotate_every_two(t):
    t1 = t[..., ::2]
    t2 = t[..., 1::2]
    return jnp.stack([-t2, t1], axis=-1).reshape(t.shape)


def _retention(q, k, v, g):
    b, s, h, dk = q.shape
    dv = v.shape[-1]
    pos = jnp.arange(s, dtype=jnp.float32)
    angle = jnp.repeat(1.0 / (10000.0 ** jnp.linspace(0.0, 1.0, dk // 2, dtype=jnp.float32)), 2)
    theta = pos[:, None] * angle[None, :]
    cos = jnp.cos(theta)[None, :, None, :]
    sin = jnp.sin(theta)[None, :, None, :]
    q = q.astype(jnp.float32)
    k = k.astype(jnp.float32) * (dk ** -0.5)
    v = v.astype(jnp.float32)
    q = q * cos + _rotate_every_two(q) * sin
    k = k * cos + _rotate_every_two(k) * sin
    log_g = jnp.log(1.0 - 2.0 ** (-5.0 - jnp.arange(h, dtype=jnp.float32)))
    C = RET_CHUNK
    n = s // C
    qc = q.reshape(b, n, C, h, dk)
    kc = k.reshape(b, n, C, h, dk)
    vc = v.reshape(b, n, C, h, dv)
    idx = jnp.arange(C, dtype=jnp.float32)
    diff = idx[:, None] - idx[None, :]
    decay = jnp.where(diff[None] >= 0, jnp.exp(jnp.maximum(diff, 0.0)[None] * log_g[:, None, None]), 0.0)
    intra = jnp.einsum('bnihd,bnjhd->bnhij', qc, kc) * decay[None, None]
    intra = jnp.einsum('bnhij,bnjhe->bnihe', intra, vc)
    zeta = jnp.exp((C - 1.0 - idx)[None, :] * log_g[:, None])
    kv = jnp.einsum('bnjhd,bnjhe,hj->nbhde', kc, vc, zeta)
    g_chunk = jnp.exp(C * log_g)[None, :, None, None]

    def step(state, kv_c):
        return g_chunk * state + kv_c, state

    _, r_prev = lax.scan(step, jnp.zeros((b, h, dk, dv), jnp.float32), kv)
    xi = jnp.exp((idx + 1.0)[:, None] * log_g[None, :])
    cross = jnp.einsum('bnihd,nbhde->bnihe', qc, r_prev) * xi[None, None, :, :, None]
    o = (intra + cross).reshape(b, s, h, dv)
    mu = jnp.mean(o, axis=-1, keepdims=True)
    var = jnp.mean(jnp.square(o - mu), axis=-1, keepdims=True)
    o = ((o - mu) * lax.rsqrt(var + EPS)).reshape(b, s, h * dv)
    return o * jax.nn.silu(g.astype(jnp.float32))


def _token_mix(h, w_in, w_out):
    b, s, _ = h.shape
    proj = h @ w_in
    aq, ak, av, rq, rk, rv, rg = jnp.split(proj, IN_SPLITS, axis=-1)
    att = _dilated_attention(aq.reshape(b, s, ATT_HEADS, ATT_HEAD_DIM),
                             ak.reshape(b, s, ATT_HEADS, ATT_HEAD_DIM),
                             av.reshape(b, s, ATT_HEADS, ATT_HEAD_DIM)).reshape(b, s, ATT_WIDTH)
    ret = _retention(rq.reshape(b, s, RET_HEADS, RET_K_DIM),
                     rk.reshape(b, s, RET_HEADS, RET_K_DIM),
                     rv.reshape(b, s, RET_HEADS, RET_V_DIM), rg)
    y = jnp.concatenate([att, ret], axis=-1).astype(h.dtype)
    return y @ w_out


def setup_inputs(seed: int = 0) -> dict:
    key = jax.random.key(seed)
    ks = jax.random.split(key, 16)
    D = D_MODEL
    f32 = jnp.float32

    def nrm(k, shape, scale):
        return jax.random.normal(k, shape, f32) * scale

    def gain(k):
        return 1.0 + 0.01 * jax.random.normal(k, (DEPTH, D), f32)

    return {
        'x': jax.random.normal(ks[0], (BATCH, SEQ, D), f32),
        'c': jax.random.normal(ks[1], (BATCH, D), f32),
        'w_ada': nrm(ks[2], (DEPTH, D, N_ADA * D), 0.5 * D ** -0.5),
        'b_ada': nrm(ks[3], (DEPTH, N_ADA * D), 0.01),
        'g_ffn1': gain(ks[4]),
        'w1_gate': nrm(ks[5], (DEPTH, D, D_FF), D ** -0.5),
        'w1_up': nrm(ks[6], (DEPTH, D, D_FF), D ** -0.5),
        'w1_down': nrm(ks[7], (DEPTH, D_FF, D), D_FF ** -0.5),
        'g_mix': gain(ks[8]),
        'w_in': nrm(ks[9], (DEPTH, D, IN_WIDTH), D ** -0.5),
        'w_out': nrm(ks[10], (DEPTH, MIX_WIDTH, D), MIX_WIDTH ** -0.5),
        'g_ffn2': gain(ks[11]),
        'w2_gate': nrm(ks[12], (DEPTH, D, D_FF), D ** -0.5),
        'w2_up': nrm(ks[13], (DEPTH, D, D_FF), D ** -0.5),
        'w2_down': nrm(ks[14], (DEPTH, D_FF, D), D_FF ** -0.5),
        'g_final': 1.0 + 0.01 * jax.random.normal(ks[15], (D,), f32),
    }


def reference(x, c, w_ada, b_ada, g_ffn1, w1_gate, w1_up, w1_down, g_mix, w_in, w_out,
              g_ffn2, w2_gate, w2_up, w2_down, g_final):
    b = x.shape[0]
    for l in range(DEPTH):
        ada = (jax.nn.silu(c) @ w_ada[l] + b_ada[l]).reshape(b, N_ADA, 1, D_MODEL)
        sh1, sc1, gt1, sh2, sc2, gt2, sh3, sc3, gt3 = [ada[:, i] for i in range(N_ADA)]
        h = _modulate(_rmsnorm(x, g_ffn1[l]), sh1, sc1)
        x = x + 0.5 * gt1 * _swiglu(h, w1_gate[l], w1_up[l], w1_down[l])
        h = _modulate(_rmsnorm(x, g_mix[l]), sh2, sc2)
        x = x + gt2 * _token_mix(h, w_in[l], w_out[l])
        h = _modulate(_rmsnorm(x, g_ffn2[l]), sh3, sc3)
        x = x + 0.5 * gt3 * _swiglu(h, w2_gate[l], w2_up[l], w2_down[l])
    return _rmsnorm(x, g_final)
```

```python
import functools

import jax
import jax.numpy as jnp
import numpy as np
from jax import lax
from jax.experimental import pallas as pl
from jax.experimental.pallas import tpu as pltpu

ATT_HEADS = 8
ATT_HEAD_DIM = 128
ATT_WIDTH = ATT_HEADS * ATT_HEAD_DIM
DILATED_CONFIGS = ((128, 1), (512, 4), (2048, 16))
RET_HEADS = 8
RET_K_DIM = 64
RET_V_DIM = 128
RET_QK_WIDTH = RET_HEADS * RET_K_DIM
RET_WIDTH = RET_HEADS * RET_V_DIM
RET_CHUNK = 128
IN_WIDTH = 3 * ATT_WIDTH + 2 * RET_QK_WIDTH + 2 * RET_WIDTH
N_ADA = 9
EPS = 1e-6

V7X_LANES = 128
V7X_SUBLANES = 8
V7X_VMEM_BYTES = 64 * 1024 * 1024

MASK_VALUE = -1e30

F32 = jnp.float32
BF16 = jnp.bfloat16


def _vmem_limit(block_bytes, scratch_bytes, temp_bytes):
    need = 2 * block_bytes + scratch_bytes + temp_bytes
    assert need <= V7X_VMEM_BYTES, need
    return int(need)


def _nbytes(shape, dtype):
    return int(np.prod(shape)) * jnp.dtype(dtype).itemsize


def _rmsnorm_modulate(x, g, shift, scale):
    y = x * lax.rsqrt(jnp.mean(x * x, axis=-1, keepdims=True) + EPS) * g
    return y * (1.0 + scale) + shift


def _silu(x):
    return x * jax.nn.sigmoid(x)


def _ada_kernel(c_ref, w_ref, b_ref, o_ref):
    s = _silu(c_ref[...])
    o_ref[...] = jnp.dot(s.astype(BF16), w_ref[...].astype(BF16), preferred_element_type=F32) + b_ref[...]


def _ada(c, w_ada, b_ada):
    b, d = c.shape
    n = w_ada.shape[1]
    tn = 1024
    assert n % tn == 0
    rows = V7X_SUBLANES
    c_pad = jnp.zeros((rows, d), F32).at[:b].set(c)
    block_bytes = _nbytes((rows, d), F32) + _nbytes((d, tn), F32) + _nbytes((1, tn), F32) + _nbytes((rows, tn), F32)
    out = pl.pallas_call(
        _ada_kernel,
        out_shape=jax.ShapeDtypeStruct((rows, n), F32),
        grid=(n // tn,),
        in_specs=[
            pl.BlockSpec((rows, d), lambda j: (0, 0)),
            pl.BlockSpec((d, tn), lambda j: (0, j)),
            pl.BlockSpec((1, tn), lambda j: (0, j)),
        ],
        out_specs=pl.BlockSpec((rows, tn), lambda j: (0, j)),
        compiler_params=pltpu.CompilerParams(
            dimension_semantics=("arbitrary",),
            vmem_limit_bytes=_vmem_limit(block_bytes, 0, _nbytes((d, tn), BF16) + _nbytes((d, tn), F32)),
        ),
        name="ada",
    )(c_pad, w_ada, b_ada.reshape(1, n))
    return out[:b].reshape(b, N_ADA, d)


def _ffn_kernel(x_ref, ada_ref, g_ref, wg_ref, wu_ref, wd_ref, gfin_ref, o_ref, h_ref, acc_ref, *, ada_row, final_norm):
    j = pl.program_id(1)

    @pl.when(j == 0)
    def _():
        shift = ada_ref[ada_row : ada_row + 1, :]
        scale = ada_ref[ada_row + 1 : ada_row + 2, :]
        h_ref[...] = _rmsnorm_modulate(x_ref[...], g_ref[...], shift, scale).astype(BF16)
        acc_ref[...] = jnp.zeros_like(acc_ref)

    h = h_ref[...]
    gate = jnp.dot(h, wg_ref[...], preferred_element_type=F32)
    up = jnp.dot(h, wu_ref[...], preferred_element_type=F32)
    act = (_silu(gate) * up).astype(BF16)
    acc_ref[...] += jnp.dot(act, wd_ref[...], preferred_element_type=F32)

    @pl.when(j == pl.num_programs(1) - 1)
    def _():
        gt = ada_ref[ada_row + 2 : ada_row + 3, :]
        out = x_ref[...] + 0.5 * gt * acc_ref[...]
        if final_norm:
            out = out * lax.rsqrt(jnp.mean(out * out, axis=-1, keepdims=True) + EPS) * gfin_ref[...]
        o_ref[...] = out


def _ffn(x, ada, g, wg, wu, wd, g_final, *, ada_row, final_norm, tm=512, tf=512):
    b, s, d = x.shape
    dff = wg.shape[1]
    tm = min(tm, s)
    assert s % tm == 0 and dff % tf == 0
    tiles_per_batch = s // tm
    x2 = x.reshape(b * s, d)
    block_bytes = (
        2 * _nbytes((tm, d), F32) + _nbytes((N_ADA, d), F32) + 2 * _nbytes((1, d), F32)
        + 2 * _nbytes((d, tf), BF16) + _nbytes((tf, d), BF16)
    )
    scratch_bytes = _nbytes((tm, d), BF16) + _nbytes((tm, d), F32)
    temp_bytes = 3 * _nbytes((tm, tf), F32) + 2 * _nbytes((tm, d), F32)
    out = pl.pallas_call(
        functools.partial(_ffn_kernel, ada_row=ada_row, final_norm=final_norm),
        out_shape=jax.ShapeDtypeStruct((b * s, d), F32),
        grid=(b * s // tm, dff // tf),
        in_specs=[
            pl.BlockSpec((tm, d), lambda i, j: (i, 0)),
            pl.BlockSpec((None, N_ADA, d), lambda i, j: (i // tiles_per_batch, 0, 0)),
            pl.BlockSpec((1, d), lambda i, j: (0, 0)),
            pl.BlockSpec((d, tf), lambda i, j: (0, j)),
            pl.BlockSpec((d, tf), lambda i, j: (0, j)),
            pl.BlockSpec((tf, d), lambda i, j: (j, 0)),
            pl.BlockSpec((1, d), lambda i, j: (0, 0)),
        ],
        out_specs=pl.BlockSpec((tm, d), lambda i, j: (i, 0)),
        scratch_shapes=[pltpu.VMEM((tm, d), BF16), pltpu.VMEM((tm, d), F32)],
        compiler_params=pltpu.CompilerParams(
            dimension_semantics=("parallel", "arbitrary"),
            vmem_limit_bytes=_vmem_limit(block_bytes, scratch_bytes, temp_bytes),
        ),
        name="ffn_final" if final_norm else "ffn",
    )(x2, ada, g.reshape(1, d), wg, wu, wd, g_final.reshape(1, d))
    return out.reshape(b, s, d)


def _inproj_kernel(x_ref, ada_ref, g_ref, w_ref, o_ref, h_ref, *, ada_row):
    @pl.when(pl.program_id(1) == 0)
    def _():
        shift = ada_ref[ada_row : ada_row + 1, :]
        scale = ada_ref[ada_row + 1 : ada_row + 2, :]
        h_ref[...] = _rmsnorm_modulate(x_ref[...], g_ref[...], shift, scale).astype(BF16)

    o_ref[...] = jnp.dot(h_ref[...], w_ref[...], preferred_element_type=F32)


def _inproj(x, ada, g, w_in, *, ada_row, tm=1024, tn=1024):
    b, s, d = x.shape
    n = w_in.shape[1]
    tm = min(tm, s)
    assert s % tm == 0 and n % tn == 0
    tiles_per_batch = s // tm
    block_bytes = (
        _nbytes((tm, d), F32) + _nbytes((N_ADA, d), F32) + _nbytes((1, d), F32)
        + _nbytes((d, tn), BF16) + _nbytes((tm, tn), F32)
    )
    out = pl.pallas_call(
        functools.partial(_inproj_kernel, ada_row=ada_row),
        out_shape=jax.ShapeDtypeStruct((b * s, n), F32),
        grid=(b * s // tm, n // tn),
        in_specs=[
            pl.BlockSpec((tm, d), lambda i, j: (i, 0)),
            pl.BlockSpec((None, N_ADA, d), lambda i, j: (i // tiles_per_batch, 0, 0)),
            pl.BlockSpec((1, d), lambda i, j: (0, 0)),
            pl.BlockSpec((d, tn), lambda i, j: (0, j)),
        ],
        out_specs=pl.BlockSpec((tm, tn), lambda i, j: (i, j)),
        scratch_shapes=[pltpu.VMEM((tm, d), BF16)],
        compiler_params=pltpu.CompilerParams(
            dimension_semantics=("parallel", "arbitrary"),
            vmem_limit_bytes=_vmem_limit(block_bytes, _nbytes((tm, d), BF16), 2 * _nbytes((tm, d), F32)),
        ),
        name="inproj",
    )(x.reshape(b * s, d), ada, g.reshape(1, d), w_in)
    return out.reshape(b, s, n)


def _strided_rows(start, size, stride):
    return pl.ds(start, size) if stride == 1 else pl.ds(start, size, stride=stride)


def _attn_kernel(q_ref, k_ref, v_ref, o_ref, ob_ref, lse_ref, bias_ref, *, seq, configs):
    dh = q_ref.shape[-1]
    n_br = len(configs)
    blk = configs[0][0] // configs[0][1]
    assert all(w // r == blk for w, r in configs)

    qi = lax.broadcasted_iota(jnp.int32, (blk, 2 * blk), 0)
    kj = lax.broadcasted_iota(jnp.int32, (blk, 2 * blk), 1)
    band = (kj >= qi) & (kj <= qi + blk)
    bias_ref[0] = jnp.where(band & (kj >= blk), 0.0, MASK_VALUE)
    bias_ref[1] = jnp.where(band, 0.0, MASK_VALUE)
    scale = dh ** -0.5

    for bi, (_, r) in enumerate(configs):
        nb = seq // (blk * r)
        assert nb * blk * r == seq

        def body(t, carry, bi=bi, r=r, nb=nb):
            rho = lax.div(t, nb)
            n = lax.rem(t, nb)
            start = rho + n * (blk * r)
            prev = rho + jnp.maximum(n - 1, 0) * (blk * r)
            if r == 1:
                start = pl.multiple_of(start, blk)
                prev = pl.multiple_of(prev, blk)
            cur_rows = _strided_rows(start, blk, r)
            prev_rows = _strided_rows(prev, blk, r)
            q = (q_ref[cur_rows, :] * scale).astype(BF16)
            kcat = jnp.concatenate([k_ref[prev_rows, :], k_ref[cur_rows, :]], axis=0).astype(BF16)
            vcat = jnp.concatenate([v_ref[prev_rows, :], v_ref[cur_rows, :]], axis=0).astype(BF16)
            s = lax.dot_general(q, kcat, (((1,), (1,)), ((), ())), preferred_element_type=F32)
            s = s + bias_ref[jnp.minimum(n, 1)]
            m = jnp.max(s, axis=-1, keepdims=True)
            p = jnp.exp(s - m)
            den = jnp.sum(p, axis=-1, keepdims=True)
            o = jnp.dot(p.astype(BF16), vcat, preferred_element_type=F32) * (1.0 / den)
            ob_ref[bi, cur_rows, :] = o
            lse_ref[bi, cur_rows, :] = jnp.broadcast_to(m + jnp.log(den), (blk, dh))
            return carry

        lax.fori_loop(0, seq // blk, body, 0)

    rows_per_step = 64

    def combine(i, carry):
        rows = pl.ds(pl.multiple_of(i * rows_per_step, rows_per_step), rows_per_step)
        lses = [lse_ref[bi, rows, :] for bi in range(n_br)]
        m = functools.reduce(jnp.maximum, lses)
        ws = [jnp.exp(l - m) for l in lses]
        den = functools.reduce(jnp.add, ws)
        num = functools.reduce(jnp.add, [w * ob_ref[bi, rows, :] for bi, w in enumerate(ws)])
        o_ref[rows, :] = (num * (1.0 / den)).astype(o_ref.dtype)
        return carry

    lax.fori_loop(0, seq // rows_per_step, combine, 0)


def _attention(proj, *, q_col, k_col, v_col):
    b, s, _ = proj.shape
    dh = ATT_HEAD_DIM
    blk = DILATED_CONFIGS[0][0] // DILATED_CONFIGS[0][1]
    n_br = len(DILATED_CONFIGS)
    block_bytes = 3 * _nbytes((s, dh), F32) + _nbytes((s, dh), BF16)
    scratch_bytes = 2 * _nbytes((n_br, s, dh), F32) + _nbytes((2, blk, 2 * blk), F32)
    temp_bytes = 8 * _nbytes((blk, 2 * blk), F32)

    def spec(col):
        return pl.BlockSpec((None, s, dh), lambda bi, h, col=col: (bi, 0, col + h))

    return pl.pallas_call(
        functools.partial(_attn_kernel, seq=s, configs=DILATED_CONFIGS),
        out_shape=jax.ShapeDtypeStruct((b, s, ATT_WIDTH), BF16),
        grid=(b, ATT_HEADS),
        in_specs=[spec(q_col), spec(k_col), spec(v_col)],
        out_specs=pl.BlockSpec((None, s, dh), lambda bi, h: (bi, 0, h)),
        scratch_shapes=[
            pltpu.VMEM((n_br, s, dh), F32),
            pltpu.VMEM((n_br, s, dh), F32),
            pltpu.VMEM((2, blk, 2 * blk), F32),
        ],
        compiler_params=pltpu.CompilerParams(
            dimension_semantics=("parallel", "arbitrary"),
            vmem_limit_bytes=_vmem_limit(block_bytes, scratch_bytes, temp_bytes),
        ),
        name="dilated_attention",
    )(proj, proj, proj)


def _rotate_pairs(t, even_lane):
    lanes = t.shape[-1]
    return jnp.where(even_lane, pltpu.roll(t, lanes - 1, 1), pltpu.roll(t, 1, 1))


def _ret_kernel(rq_ref, rk_ref, rv_ref, rg_ref, cos_ref, sin_ref, decay_ref, zeta_ref, xi_ref, gch_ref,
                o_ref, state_ref, *, chunks):
    c_len = RET_CHUNK
    dv = RET_V_DIM
    pair_lanes = 2 * RET_K_DIM
    assert pair_lanes == V7X_LANES

    @pl.when(pl.program_id(1) == 0)
    def _():
        state_ref[...] = jnp.zeros_like(state_ref)

    lane = lax.broadcasted_iota(jnp.int32, (c_len, pair_lanes), 1)
    even_lane = (lane % 2) == 0
    first_head_lane = lane < RET_K_DIM
    first_head_row = lax.broadcasted_iota(jnp.int32, (pair_lanes, dv), 0) < RET_K_DIM
    k_scale = RET_K_DIM ** -0.5

    for c in range(chunks):
        rows = slice(c * c_len, (c + 1) * c_len)
        cosv = cos_ref[rows, :]
        sinv = sin_ref[rows, :]
        for p in range(RET_HEADS // 2):
            cols = slice(p * pair_lanes, (p + 1) * pair_lanes)
            q = rq_ref[rows, cols]
            k = rk_ref[rows, cols] * k_scale
            q = q * cosv + _rotate_pairs(q, even_lane) * sinv
            k = k * cosv + _rotate_pairs(k, even_lane) * sinv
            qx = q * xi_ref[:, cols]
            k_b = k.astype(BF16)
            state = state_ref[p]
            state_b = state.astype(BF16)
            kvs = []
            for a in range(2):
                h = 2 * p + a
                own = first_head_lane if a == 0 else jnp.logical_not(first_head_lane)
                vcols = slice(h * dv, (h + 1) * dv)
                qm = jnp.where(own, q, 0.0).astype(BF16)
                qxm = jnp.where(own, qx, 0.0).astype(BF16)
                scores = lax.dot_general(qm, k_b, (((1,), (1,)), ((), ())), preferred_element_type=F32)
                scores = scores * decay_ref[h]
                v = rv_ref[rows, vcols]
                lhs = jnp.concatenate([scores.astype(BF16), qxm], axis=1)
                rhs = jnp.concatenate([v.astype(BF16), state_b], axis=0)
                o = jnp.dot(lhs, rhs, preferred_element_type=F32)
                mu = jnp.mean(o, axis=-1, keepdims=True)
                dev = o - mu
                var = jnp.mean(dev * dev, axis=-1, keepdims=True)
                g = rg_ref[rows, vcols]
                o_ref[rows, vcols] = (dev * lax.rsqrt(var + EPS) * _silu(g)).astype(o_ref.dtype)
                vz = (v * zeta_ref[h]).astype(BF16)
                kvs.append(lax.dot_general(k_b, vz, (((0,), (0,)), ((), ())), preferred_element_type=F32))
            state_ref[p] = gch_ref[p] * state + jnp.where(first_head_row, kvs[0], kvs[1])


def _retention_tables(s):
    dk, c_len, h = RET_K_DIM, RET_CHUNK, RET_HEADS
    pos = jnp.arange(s, dtype=F32)
    angle = jnp.repeat(1.0 / (10000.0 ** jnp.linspace(0.0, 1.0, dk // 2, dtype=F32)), 2)
    theta = pos[:, None] * angle[None, :]
    sign = jnp.tile(jnp.array([-1.0, 1.0], F32), dk // 2)
    cos = jnp.tile(jnp.cos(theta), (1, 2))
    sin = jnp.tile(jnp.sin(theta) * sign[None, :], (1, 2))
    log_g = jnp.log(1.0 - 2.0 ** (-5.0 - jnp.arange(h, dtype=F32)))
    idx = jnp.arange(c_len, dtype=F32)
    diff = idx[:, None] - idx[None, :]
    decay = jnp.where(diff[None] >= 0, jnp.exp(jnp.maximum(diff, 0.0)[None] * log_g[:, None, None]), 0.0)
    zeta = jnp.exp((c_len - 1.0 - idx)[None, :] * log_g[:, None])
    zeta_b = jnp.broadcast_to(zeta[:, :, None], (h, c_len, RET_V_DIM))
    xi = jnp.exp((idx + 1.0)[:, None] * log_g[None, :])
    xi_q = jnp.repeat(xi, dk, axis=1)
    g_chunk = jnp.exp(c_len * log_g)
    gch = jnp.broadcast_to(jnp.repeat(g_chunk, dk).reshape(h // 2, 2 * dk, 1), (h // 2, 2 * dk, RET_V_DIM))
    return cos, sin, decay, zeta_b, xi_q, gch


def _retention(proj, *, q_col, k_col, v_col, g_col, tm=512):
    b, s, _ = proj.shape
    tm = min(tm, s)
    assert s % tm == 0 and tm % RET_CHUNK == 0
    qk_w, v_w = RET_QK_WIDTH, RET_WIDTH
    assert q_col % qk_w == 0 and k_col % qk_w == 0 and v_col % v_w == 0 and g_col % v_w == 0
    cos, sin, decay, zeta_b, xi_q, gch = _retention_tables(s)
    block_bytes = (
        2 * _nbytes((tm, qk_w), F32) + 2 * _nbytes((tm, v_w), F32) + 2 * _nbytes((tm, V7X_LANES), F32)
        + decay.nbytes + zeta_b.nbytes + xi_q.nbytes + gch.nbytes + _nbytes((tm, v_w), BF16)
    )
    full = lambda arr: pl.BlockSpec(arr.shape, lambda bi, t: (0,) * arr.ndim)
    return pl.pallas_call(
        functools.partial(_ret_kernel, chunks=tm // RET_CHUNK),
        out_shape=jax.ShapeDtypeStruct((b, s, v_w), BF16),
        grid=(b, s // tm),
        in_specs=[
            pl.BlockSpec((None, tm, qk_w), lambda bi, t: (bi, t, q_col // qk_w)),
            pl.BlockSpec((None, tm, qk_w), lambda bi, t: (bi, t, k_col // qk_w)),
            pl.BlockSpec((None, tm, v_w), lambda bi, t: (bi, t, v_col // v_w)),
            pl.BlockSpec((None, tm, v_w), lambda bi, t: (bi, t, g_col // v_w)),
            pl.BlockSpec((tm, V7X_LANES), lambda bi, t: (t, 0)),
            pl.BlockSpec((tm, V7X_LANES), lambda bi, t: (t, 0)),
            full(decay), full(zeta_b), full(xi_q), full(gch),
        ],
        out_specs=pl.BlockSpec((None, tm, v_w), lambda bi, t: (bi, t, 0)),
        scratch_shapes=[pltpu.VMEM((RET_HEADS // 2, 2 * RET_K_DIM, RET_V_DIM), F32)],
        compiler_params=pltpu.CompilerParams(
            dimension_semantics=("parallel", "arbitrary"),
            vmem_limit_bytes=_vmem_limit(block_bytes, gch.nbytes, 16 * _nbytes((RET_CHUNK, 2 * RET_CHUNK), F32)),
        ),
        name="retention",
    )(proj, proj, proj, proj, cos, sin, decay, zeta_b, xi_q, gch)


def _outproj_kernel(x_ref, ada_ref, att_ref, ret_ref, wa_ref, wr_ref, o_ref, *, ada_row):
    y = jnp.dot(att_ref[...], wa_ref[...], preferred_element_type=F32)
    y = y + jnp.dot(ret_ref[...], wr_ref[...], preferred_element_type=F32)
    o_ref[...] = x_ref[...] + ada_ref[ada_row : ada_row + 1, :] * y


def _outproj(x, ada, att, ret, w_att, w_ret, *, ada_row, tm=512):
    b, s, d = x.shape
    tm = min(tm, s)
    assert s % tm == 0
    tiles_per_batch = s // tm
    wa, wr = att.shape[-1], ret.shape[-1]
    block_bytes = (
        2 * _nbytes((tm, d), F32) + _nbytes((N_ADA, d), F32) + _nbytes((tm, wa), BF16) + _nbytes((tm, wr), BF16)
        + _nbytes((wa, d), BF16) + _nbytes((wr, d), BF16)
    )
    out = pl.pallas_call(
        functools.partial(_outproj_kernel, ada_row=ada_row),
        out_shape=jax.ShapeDtypeStruct((b * s, d), F32),
        grid=(b * s // tm,),
        in_specs=[
            pl.BlockSpec((tm, d), lambda i: (i, 0)),
            pl.BlockSpec((None, N_ADA, d), lambda i: (i // tiles_per_batch, 0, 0)),
            pl.BlockSpec((tm, wa), lambda i: (i, 0)),
            pl.BlockSpec((tm, wr), lambda i: (i, 0)),
            pl.BlockSpec((wa, d), lambda i: (0, 0)),
            pl.BlockSpec((wr, d), lambda i: (0, 0)),
        ],
        out_specs=pl.BlockSpec((tm, d), lambda i: (i, 0)),
        compiler_params=pltpu.CompilerParams(
            dimension_semantics=("parallel",),
            vmem_limit_bytes=_vmem_limit(block_bytes, 0, 2 * _nbytes((tm, d), F32)),
        ),
        name="outproj",
    )(x.reshape(b * s, d), ada, att.reshape(b * s, wa), ret.reshape(b * s, wr), w_att, w_ret)
    return out.reshape(b, s, d)


def kernel(x, c, w_ada, b_ada, g_ffn1, w1_gate, w1_up, w1_down, g_mix, w_in, w_out, g_ffn2, w2_gate, w2_up, w2_down, g_final):
    depth = w_ada.shape[0]
    att_w = ATT_WIDTH
    for l in range(depth):
        last = l == depth - 1
        ada = _ada(c, w_ada[l], b_ada[l])
        x = _ffn(x, ada, g_ffn1[l], w1_gate[l].astype(BF16), w1_up[l].astype(BF16), w1_down[l].astype(BF16),
                 g_final, ada_row=0, final_norm=False)
        proj = _inproj(x, ada, g_mix[l], w_in[l].astype(BF16), ada_row=3)
        lane_blocks = att_w // V7X_LANES
        att = _attention(proj, q_col=0, k_col=lane_blocks, v_col=2 * lane_blocks)
        ret = _retention(proj, q_col=3 * att_w, k_col=3 * att_w + RET_QK_WIDTH,
                         v_col=3 * att_w + 2 * RET_QK_WIDTH, g_col=3 * att_w + 2 * RET_QK_WIDTH + RET_WIDTH)
        w_o = w_out[l].astype(BF16)
        x = _outproj(x, ada, att, ret, w_o[:att_w], w_o[att_w:], ada_row=5)
        x = _ffn(x, ada, g_ffn2[l], w2_gate[l].astype(BF16), w2_up[l].astype(BF16), w2_down[l].astype(BF16),
                 g_final, ada_row=6, final_norm=last)
    if depth == 0:
        raise ValueError("depth must be >= 1")
    return x
```

```python
import functools

import jax
import jax.numpy as jnp
import numpy as np
from jax import lax
from jax.experimental import pallas as pl
from jax.experimental.pallas import tpu as pltpu

ATT_HEADS = 8
ATT_HEAD_DIM = 128
ATT_WIDTH = ATT_HEADS * ATT_HEAD_DIM
DILATED_CONFIGS = ((128, 1), (512, 4), (2048, 16))
RET_HEADS = 8
RET_K_DIM = 64
RET_V_DIM = 128
RET_QK_WIDTH = RET_HEADS * RET_K_DIM
RET_WIDTH = RET_HEADS * RET_V_DIM
RET_CHUNK = 128
IN_WIDTH = 3 * ATT_WIDTH + 2 * RET_QK_WIDTH + 2 * RET_WIDTH
N_ADA = 9
EPS = 1e-6

V7X_LANES = 128
V7X_SUBLANES = 8
V7X_VMEM_BYTES = 64 * 1024 * 1024

MASK_VALUE = -1e30

F32 = jnp.float32
BF16 = jnp.bfloat16


def _vmem_limit(block_bytes, scratch_bytes, temp_bytes):
    need = 2 * block_bytes + scratch_bytes + temp_bytes
    assert need <= V7X_VMEM_BYTES, need
    return int(need)


def _nbytes(shape, dtype):
    return int(np.prod(shape)) * jnp.dtype(dtype).itemsize


def _rmsnorm_modulate(x, g, shift, scale):
    y = x * lax.rsqrt(jnp.mean(x * x, axis=-1, keepdims=True) + EPS) * g
    return y * (1.0 + scale) + shift


def _silu(x):
    return x * jax.nn.sigmoid(x)


def _ada_kernel(c_ref, w_ref, b_ref, o_ref):
    s = _silu(c_ref[...])
    o_ref[...] = jnp.dot(s.astype(BF16), w_ref[...].astype(BF16), preferred_element_type=F32) + b_ref[...]


def _ada(c, w_ada, b_ada):
    b, d = c.shape
    n = w_ada.shape[1]
    tn = 1024
    assert n % tn == 0
    rows = V7X_SUBLANES
    c_pad = jnp.zeros((rows, d), F32).at[:b].set(c)
    block_bytes = _nbytes((rows, d), F32) + _nbytes((d, tn), F32) + _nbytes((1, tn), F32) + _nbytes((rows, tn), F32)
    out = pl.pallas_call(
        _ada_kernel,
        out_shape=jax.ShapeDtypeStruct((rows, n), F32),
        grid=(n // tn,),
        in_specs=[
            pl.BlockSpec((rows, d), lambda j: (0, 0)),
            pl.BlockSpec((d, tn), lambda j: (0, j)),
            pl.BlockSpec((1, tn), lambda j: (0, j)),
        ],
        out_specs=pl.BlockSpec((rows, tn), lambda j: (0, j)),
        compiler_params=pltpu.CompilerParams(
            dimension_semantics=("arbitrary",),
            vmem_limit_bytes=_vmem_limit(block_bytes, 0, _nbytes((d, tn), BF16) + _nbytes((d, tn), F32)),
        ),
        name="ada",
    )(c_pad, w_ada, b_ada.reshape(1, n))
    return out[:b].reshape(b, N_ADA, d)


def _ffn_kernel(x_ref, ada_ref, g_ref, wg_ref, wu_ref, wd_ref, gfin_ref, o_ref, h_ref, acc_ref, *, ada_row, final_norm):
    j = pl.program_id(1)

    @pl.when(j == 0)
    def _():
        shift = ada_ref[ada_row : ada_row + 1, :]
        scale = ada_ref[ada_row + 1 : ada_row + 2, :]
        h_ref[...] = _rmsnorm_modulate(x_ref[...], g_ref[...], shift, scale).astype(BF16)
        acc_ref[...] = jnp.zeros_like(acc_ref)

    h = h_ref[...]
    gate = jnp.dot(h, wg_ref[...], preferred_element_type=F32)
    up = jnp.dot(h, wu_ref[...], preferred_element_type=F32)
    act = (_silu(gate) * up).astype(BF16)
    acc_ref[...] += jnp.dot(act, wd_ref[...], preferred_element_type=F32)

    @pl.when(j == pl.num_programs(1) - 1)
    def _():
        gt = ada_ref[ada_row + 2 : ada_row + 3, :]
        out = x_ref[...] + 0.5 * gt * acc_ref[...]
        if final_norm:
            out = out * lax.rsqrt(jnp.mean(out * out, axis=-1, keepdims=True) + EPS) * gfin_ref[...]
        o_ref[...] = out


def _ffn(x, ada, g, wg, wu, wd, g_final, *, ada_row, final_norm, tm=512, tf=512):
    b, s, d = x.shape
    dff = wg.shape[1]
    tm = min(tm, s)
    assert s % tm == 0 and dff % tf == 0
    tiles_per_batch = s // tm
    x2 = x.reshape(b * s, d)
    block_bytes = (
        2 * _nbytes((tm, d), F32) + _nbytes((N_ADA, d), F32) + 2 * _nbytes((1, d), F32)
        + 2 * _nbytes((d, tf), BF16) + _nbytes((tf, d), BF16)
    )
    scratch_bytes = _nbytes((tm, d), BF16) + _nbytes((tm, d), F32)
    temp_bytes = 3 * _nbytes((tm, tf), F32) + 2 * _nbytes((tm, d), F32)
    out = pl.pallas_call(
        functools.partial(_ffn_kernel, ada_row=ada_row, final_norm=final_norm),
        out_shape=jax.ShapeDtypeStruct((b * s, d), F32),
        grid=(b * s // tm, dff // tf),
        in_specs=[
            pl.BlockSpec((tm, d), lambda i, j: (i, 0)),
            pl.BlockSpec((None, N_ADA, d), lambda i, j: (i // tiles_per_batch, 0, 0)),
            pl.BlockSpec((1, d), lambda i, j: (0, 0)),
            pl.BlockSpec((d, tf), lambda i, j: (0, j)),
            pl.BlockSpec((d, tf), lambda i, j: (0, j)),
            pl.BlockSpec((tf, d), lambda i, j: (j, 0)),
            pl.BlockSpec((1, d), lambda i, j: (0, 0)),
        ],
        out_specs=pl.BlockSpec((tm, d), lambda i, j: (i, 0)),
        scratch_shapes=[pltpu.VMEM((tm, d), BF16), pltpu.VMEM((tm, d), F32)],
        compiler_params=pltpu.CompilerParams(
            dimension_semantics=("parallel", "arbitrary"),
            vmem_limit_bytes=_vmem_limit(block_bytes, scratch_bytes, temp_bytes),
        ),
        name="ffn_final" if final_norm else "ffn",
    )(x2, ada, g.reshape(1, d), wg, wu, wd, g_final.reshape(1, d))
    return out.reshape(b, s, d)


def _inproj_kernel(x_ref, ada_ref, g_ref, w_ref, o_ref, h_ref, *, ada_row):
    @pl.when(pl.program_id(1) == 0)
    def _():
        shift = ada_ref[ada_row : ada_row + 1, :]
        scale = ada_ref[ada_row + 1 : ada_row + 2, :]
        h_ref[...] = _rmsnorm_modulate(x_ref[...], g_ref[...], shift, scale).astype(BF16)

    o_ref[...] = jnp.dot(h_ref[...], w_ref[...], preferred_element_type=F32)


def _inproj(x, ada, g, w_in, *, ada_row, tm=1024, tn=1024):
    b, s, d = x.shape
    n = w_in.shape[1]
    tm = min(tm, s)
    assert s % tm == 0 and n % tn == 0
    tiles_per_batch = s // tm
    block_bytes = (
        _nbytes((tm, d), F32) + _nbytes((N_ADA, d), F32) + _nbytes((1, d), F32)
        + _nbytes((d, tn), BF16) + _nbytes((tm, tn), F32)
    )
    out = pl.pallas_call(
        functools.partial(_inproj_kernel, ada_row=ada_row),
        out_shape=jax.ShapeDtypeStruct((b * s, n), F32),
        grid=(b * s // tm, n // tn),
        in_specs=[
            pl.BlockSpec((tm, d), lambda i, j: (i, 0)),
            pl.BlockSpec((None, N_ADA, d), lambda i, j: (i // tiles_per_batch, 0, 0)),
            pl.BlockSpec((1, d), lambda i, j: (0, 0)),
            pl.BlockSpec((d, tn), lambda i, j: (0, j)),
        ],
        out_specs=pl.BlockSpec((tm, tn), lambda i, j: (i, j)),
        scratch_shapes=[pltpu.VMEM((tm, d), BF16)],
        compiler_params=pltpu.CompilerParams(
            dimension_semantics=("parallel", "arbitrary"),
            vmem_limit_bytes=_vmem_limit(block_bytes, _nbytes((tm, d), BF16), 2 * _nbytes((tm, d), F32)),
        ),
        name="inproj",
    )(x.reshape(b * s, d), ada, g.reshape(1, d), w_in)
    return out.reshape(b, s, n)


LOG2_E = 1.4426950408889634


def _attn_kernel(q_ref, k_ref, v_ref, o_ref, xf_ref, xb_ref, ob_ref, m_ref, d_ref, bias_ref, *,
                 seq, blk, dilations, unroll):
    dh = q_ref.shape[-1]
    n_br = len(dilations)
    n_blk = seq // blk
    in_refs = (q_ref, k_ref, v_ref)
    q_scale = dh ** -0.5 * LOG2_E

    qi = lax.broadcasted_iota(jnp.int32, (blk, 2 * blk), 0)
    kj = lax.broadcasted_iota(jnp.int32, (blk, 2 * blk), 1)
    band = (kj >= qi) & (kj <= qi + blk)
    bias_ref[0] = jnp.where(band & (kj >= blk), 0.0, MASK_VALUE)
    bias_ref[1] = jnp.where(band, 0.0, MASK_VALUE)
    for li in range(n_br):
        for ti in (1, 2):
            xb_ref[li, ti, 0:blk, :] = jnp.zeros((blk, dh), BF16)

    def block_rows(t, offset=0):
        return pl.ds(pl.multiple_of(t * blk + offset, blk), blk)

    def to_mxu(x, ti):
        return (x * q_scale if ti == 0 else x).astype(BF16)

    def coarser_rows(t, li):
        r_prev, r = dilations[li - 1], dilations[li]
        f = r // r_prev
        blocks_per_seg = seq // (r * blk)
        seg = lax.div(t, blocks_per_seg)
        c = lax.rem(t, blocks_per_seg)
        start = lax.rem(seg, r_prev) * (seq // r_prev) + lax.div(seg, r_prev) + c * (blk * f)
        return pl.ds(start, blk, stride=f)

    def cast_natural(t, carry):
        for ti, ref in enumerate(in_refs):
            xb_ref[0, ti, block_rows(t, blk), :] = to_mxu(ref[block_rows(t), :], ti)
        return carry

    lax.fori_loop(0, n_blk, cast_natural, 0, unroll=unroll)

    for li in range(1, n_br):
        def gather(t, carry, li=li):
            src = coarser_rows(t, li)
            for ti, ref in enumerate(in_refs):
                x = ref[src, :] if li == 1 else xf_ref[li - 2, ti, src, :]
                if li < n_br - 1:
                    xf_ref[li - 1, ti, block_rows(t), :] = x
                xb_ref[li, ti, block_rows(t, blk), :] = to_mxu(x, ti)
            return carry

        lax.fori_loop(0, n_blk, gather, 0, unroll=unroll)

    for li, r in enumerate(dilations):
        def block(t, carry, li=li, blocks_per_seg=seq // (r * blk)):
            n = lax.rem(t, blocks_per_seg)
            kv_rows = pl.ds(pl.multiple_of(t * blk, blk), 2 * blk)
            q = xb_ref[li, 0, block_rows(t, blk), :]
            s = lax.dot_general(q, xb_ref[li, 1, kv_rows, :], (((1,), (1,)), ((), ())), preferred_element_type=F32)
            s = s + bias_ref[jnp.minimum(n, 1)]
            m = jnp.max(s, axis=-1, keepdims=True)
            p = jnp.exp2(s - m)
            ob_ref[li, block_rows(t), :] = jnp.dot(p.astype(BF16), xb_ref[li, 2, kv_rows, :], preferred_element_type=F32)
            m_ref[li, block_rows(t), :] = jnp.broadcast_to(m, (blk, dh))
            d_ref[li, block_rows(t), :] = jnp.broadcast_to(jnp.sum(p, axis=-1, keepdims=True), (blk, dh))
            return carry

        lax.fori_loop(0, n_blk, block, 0, unroll=32)

    for li in range(n_br - 1, 0, -1):
        def merge(t, carry, li=li):
            src = block_rows(t)
            dst = coarser_rows(t, li)
            m_a, m_b = m_ref[li, src, :], m_ref[li - 1, dst, :]
            m = jnp.maximum(m_a, m_b)
            w_a, w_b = jnp.exp2(m_a - m), jnp.exp2(m_b - m)
            o = w_a * ob_ref[li, src, :] + w_b * ob_ref[li - 1, dst, :]
            den = w_a * d_ref[li, src, :] + w_b * d_ref[li - 1, dst, :]
            if li > 1:
                ob_ref[li - 1, dst, :] = o
                m_ref[li - 1, dst, :] = m
                d_ref[li - 1, dst, :] = den
            else:
                o_ref[dst, :] = o * (1.0 / den)
            return carry

        lax.fori_loop(0, n_blk, merge, 0, unroll=unroll)


def _attention(proj, *, q_col, k_col, v_col):
    b, s, _ = proj.shape
    dh = ATT_HEAD_DIM
    blk = DILATED_CONFIGS[0][0] // DILATED_CONFIGS[0][1]
    dilations = tuple(r for _, r in DILATED_CONFIGS)
    n_br = len(dilations)
    assert n_br >= 2 and dilations[0] == 1 and all(w // r == blk for w, r in DILATED_CONFIGS)
    assert all(dilations[i + 1] % dilations[i] == 0 for i in range(n_br - 1))
    assert all(s % (r * blk) == 0 for r in dilations)
    unroll = 4
    scratch = [
        pltpu.VMEM((max(n_br - 2, 1), 3, s, dh), F32),
        pltpu.VMEM((n_br, 3, blk + s, dh), BF16),
        pltpu.VMEM((n_br, s, dh), F32),
        pltpu.VMEM((n_br, s, dh), F32),
        pltpu.VMEM((n_br, s, dh), F32),
        pltpu.VMEM((2, blk, 2 * blk), F32),
    ]
    block_bytes = 4 * _nbytes((s, dh), F32)
    scratch_bytes = sum(_nbytes(v.shape, v.dtype) for v in scratch)
    temp_bytes = 8 * unroll * _nbytes((blk, 2 * blk), F32)

    def spec(col):
        return pl.BlockSpec((None, s, dh), lambda bi, h, col=col: (bi, 0, col + h))

    return pl.pallas_call(
        functools.partial(_attn_kernel, seq=s, blk=blk, dilations=dilations, unroll=unroll),
        out_shape=jax.ShapeDtypeStruct((b, s, ATT_WIDTH), F32),
        grid=(b, ATT_HEADS),
        in_specs=[spec(q_col), spec(k_col), spec(v_col)],
        out_specs=pl.BlockSpec((None, s, dh), lambda bi, h: (bi, 0, h)),
        scratch_shapes=scratch,
        compiler_params=pltpu.CompilerParams(
            dimension_semantics=("parallel", "arbitrary"),
            vmem_limit_bytes=_vmem_limit(block_bytes, scratch_bytes, temp_bytes),
        ),
        name="dilated_attention",
    )(proj, proj, proj)


def _rotate_pairs(t, even_lane):
    lanes = t.shape[-1]
    return jnp.where(even_lane, pltpu.roll(t, lanes - 1, 1), pltpu.roll(t, 1, 1))


def _ret_kernel(rq_ref, rk_ref, rv_ref, rg_ref, cos_ref, sin_ref, decay_ref, zeta_ref, xi_ref, gch_ref,
                o_ref, state_ref, *, chunks):
    c_len = RET_CHUNK
    dv = RET_V_DIM
    pair_lanes = 2 * RET_K_DIM
    assert pair_lanes == V7X_LANES

    @pl.when(pl.program_id(1) == 0)
    def _():
        state_ref[...] = jnp.zeros_like(state_ref)

    lane = lax.broadcasted_iota(jnp.int32, (c_len, pair_lanes), 1)
    even_lane = (lane % 2) == 0
    first_head_lane = lane < RET_K_DIM
    first_head_row = lax.broadcasted_iota(jnp.int32, (pair_lanes, dv), 0) < RET_K_DIM
    k_scale = RET_K_DIM ** -0.5

    for c in range(chunks):
        rows = slice(c * c_len, (c + 1) * c_len)
        cosv = cos_ref[rows, :]
        sinv = sin_ref[rows, :]
        for p in range(RET_HEADS // 2):
            cols = slice(p * pair_lanes, (p + 1) * pair_lanes)
            q = rq_ref[rows, cols]
            k = rk_ref[rows, cols] * k_scale
            q = q * cosv + _rotate_pairs(q, even_lane) * sinv
            k = k * cosv + _rotate_pairs(k, even_lane) * sinv
            qx = q * xi_ref[:, cols]
            k_b = k.astype(BF16)
            state = state_ref[p]
            state_b = state.astype(BF16)
            kvs = []
            for a in range(2):
                h = 2 * p + a
                own = first_head_lane if a == 0 else jnp.logical_not(first_head_lane)
                vcols = slice(h * dv, (h + 1) * dv)
                qm = jnp.where(own, q, 0.0).astype(BF16)
                qxm = jnp.where(own, qx, 0.0).astype(BF16)
                scores = lax.dot_general(qm, k_b, (((1,), (1,)), ((), ())), preferred_element_type=F32)
                scores = scores * decay_ref[h]
                v = rv_ref[rows, vcols]
                lhs = jnp.concatenate([scores.astype(BF16), qxm], axis=1)
                rhs = jnp.concatenate([v.astype(BF16), state_b], axis=0)
                o = jnp.dot(lhs, rhs, preferred_element_type=F32)
                mu = jnp.mean(o, axis=-1, keepdims=True)
                dev = o - mu
                var = jnp.mean(dev * dev, axis=-1, keepdims=True)
                g = rg_ref[rows, vcols]
                o_ref[rows, vcols] = (dev * lax.rsqrt(var + EPS) * _silu(g)).astype(o_ref.dtype)
                vz = (v * zeta_ref[h]).astype(BF16)
                kvs.append(lax.dot_general(k_b, vz, (((0,), (0,)), ((), ())), preferred_element_type=F32))
            state_ref[p] = gch_ref[p] * state + jnp.where(first_head_row, kvs[0], kvs[1])


def _retention_tables(s):
    dk, c_len, h = RET_K_DIM, RET_CHUNK, RET_HEADS
    pos = jnp.arange(s, dtype=F32)
    angle = jnp.repeat(1.0 / (10000.0 ** jnp.linspace(0.0, 1.0, dk // 2, dtype=F32)), 2)
    theta = pos[:, None] * angle[None, :]
    sign = jnp.tile(jnp.array([-1.0, 1.0], F32), dk // 2)
    cos = jnp.tile(jnp.cos(theta), (1, 2))
    sin = jnp.tile(jnp.sin(theta) * sign[None, :], (1, 2))
    log_g = jnp.log(1.0 - 2.0 ** (-5.0 - jnp.arange(h, dtype=F32)))
    idx = jnp.arange(c_len, dtype=F32)
    diff = idx[:, None] - idx[None, :]
    decay = jnp.where(diff[None] >= 0, jnp.exp(jnp.maximum(diff, 0.0)[None] * log_g[:, None, None]), 0.0)
    zeta = jnp.exp((c_len - 1.0 - idx)[None, :] * log_g[:, None])
    zeta_b = jnp.broadcast_to(zeta[:, :, None], (h, c_len, RET_V_DIM))
    xi = jnp.exp((idx + 1.0)[:, None] * log_g[None, :])
    xi_q = jnp.repeat(xi, dk, axis=1)
    g_chunk = jnp.exp(c_len * log_g)
    gch = jnp.broadcast_to(jnp.repeat(g_chunk, dk).reshape(h // 2, 2 * dk, 1), (h // 2, 2 * dk, RET_V_DIM))
    return cos, sin, decay, zeta_b, xi_q, gch


def _retention(proj, *, q_col, k_col, v_col, g_col, tm=512):
    b, s, _ = proj.shape
    tm = min(tm, s)
    assert s % tm == 0 and tm % RET_CHUNK == 0
    qk_w, v_w = RET_QK_WIDTH, RET_WIDTH
    assert q_col % qk_w == 0 and k_col % qk_w == 0 and v_col % v_w == 0 and g_col % v_w == 0
    cos, sin, decay, zeta_b, xi_q, gch = _retention_tables(s)
    block_bytes = (
        2 * _nbytes((tm, qk_w), F32) + 2 * _nbytes((tm, v_w), F32) + 2 * _nbytes((tm, V7X_LANES), F32)
        + decay.nbytes + zeta_b.nbytes + xi_q.nbytes + gch.nbytes + _nbytes((tm, v_w), BF16)
    )
    full = lambda arr: pl.BlockSpec(arr.shape, lambda bi, t: (0,) * arr.ndim)
    return pl.pallas_call(
        functools.partial(_ret_kernel, chunks=tm // RET_CHUNK),
        out_shape=jax.ShapeDtypeStruct((b, s, v_w), BF16),
        grid=(b, s // tm),
        in_specs=[
            pl.BlockSpec((None, tm, qk_w), lambda bi, t: (bi, t, q_col // qk_w)),
            pl.BlockSpec((None, tm, qk_w), lambda bi, t: (bi, t, k_col // qk_w)),
            pl.BlockSpec((None, tm, v_w), lambda bi, t: (bi, t, v_col // v_w)),
            pl.BlockSpec((None, tm, v_w), lambda bi, t: (bi, t, g_col // v_w)),
            pl.BlockSpec((tm, V7X_LANES), lambda bi, t: (t, 0)),
            pl.BlockSpec((tm, V7X_LANES), lambda bi, t: (t, 0)),
            full(decay), full(zeta_b), full(xi_q), full(gch),
        ],
        out_specs=pl.BlockSpec((None, tm, v_w), lambda bi, t: (bi, t, 0)),
        scratch_shapes=[pltpu.VMEM((RET_HEADS // 2, 2 * RET_K_DIM, RET_V_DIM), F32)],
        compiler_params=pltpu.CompilerParams(
            dimension_semantics=("parallel", "arbitrary"),
            vmem_limit_bytes=_vmem_limit(block_bytes, gch.nbytes, 16 * _nbytes((RET_CHUNK, 2 * RET_CHUNK), F32)),
        ),
        name="retention",
    )(proj, proj, proj, proj, cos, sin, decay, zeta_b, xi_q, gch)


def _outproj_kernel(x_ref, ada_ref, att_ref, ret_ref, wa_ref, wr_ref, o_ref, *, ada_row):
    y = jnp.dot(att_ref[...].astype(BF16), wa_ref[...], preferred_element_type=F32)
    y = y + jnp.dot(ret_ref[...], wr_ref[...], preferred_element_type=F32)
    o_ref[...] = x_ref[...] + ada_ref[ada_row : ada_row + 1, :] * y


def _outproj(x, ada, att, ret, w_att, w_ret, *, ada_row, tm=512):
    b, s, d = x.shape
    tm = min(tm, s)
    assert s % tm == 0
    tiles_per_batch = s // tm
    wa, wr = att.shape[-1], ret.shape[-1]
    block_bytes = (
        2 * _nbytes((tm, d), F32) + _nbytes((N_ADA, d), F32) + _nbytes((tm, wa), att.dtype) + _nbytes((tm, wr), ret.dtype)
        + _nbytes((wa, d), BF16) + _nbytes((wr, d), BF16)
    )
    out = pl.pallas_call(
        functools.partial(_outproj_kernel, ada_row=ada_row),
        out_shape=jax.ShapeDtypeStruct((b * s, d), F32),
        grid=(b * s // tm,),
        in_specs=[
            pl.BlockSpec((tm, d), lambda i: (i, 0)),
            pl.BlockSpec((None, N_ADA, d), lambda i: (i // tiles_per_batch, 0, 0)),
            pl.BlockSpec((tm, wa), lambda i: (i, 0)),
            pl.BlockSpec((tm, wr), lambda i: (i, 0)),
            pl.BlockSpec((wa, d), lambda i: (0, 0)),
            pl.BlockSpec((wr, d), lambda i: (0, 0)),
        ],
        out_specs=pl.BlockSpec((tm, d), lambda i: (i, 0)),
        compiler_params=pltpu.CompilerParams(
            dimension_semantics=("parallel",),
            vmem_limit_bytes=_vmem_limit(block_bytes, 0, 2 * _nbytes((tm, d), F32)),
        ),
        name="outproj",
    )(x.reshape(b * s, d), ada, att.reshape(b * s, wa), ret.reshape(b * s, wr), w_att, w_ret)
    return out.reshape(b, s, d)


def kernel(x, c, w_ada, b_ada, g_ffn1, w1_gate, w1_up, w1_down, g_mix, w_in, w_out, g_ffn2, w2_gate, w2_up, w2_down, g_final):
    depth = w_ada.shape[0]
    att_w = ATT_WIDTH
    for l in range(depth):
        last = l == depth - 1
        ada = _ada(c, w_ada[l], b_ada[l])
        x = _ffn(x, ada, g_ffn1[l], w1_gate[l].astype(BF16), w1_up[l].astype(BF16), w1_down[l].astype(BF16),
                 g_final, ada_row=0, final_norm=False)
        proj = _inproj(x, ada, g_mix[l], w_in[l].astype(BF16), ada_row=3)
        lane_blocks = att_w // V7X_LANES
        att = _attention(proj, q_col=0, k_col=lane_blocks, v_col=2 * lane_blocks)
        ret = _retention(proj, q_col=3 * att_w, k_col=3 * att_w + RET_QK_WIDTH,
                         v_col=3 * att_w + 2 * RET_QK_WIDTH, g_col=3 * att_w + 2 * RET_QK_WIDTH + RET_WIDTH)
        w_o = w_out[l].astype(BF16)
        x = _outproj(x, ada, att, ret, w_o[:att_w], w_o[att_w:], ada_row=5)
        x = _ffn(x, ada, g_ffn2[l], w2_gate[l].astype(BF16), w2_up[l].astype(BF16), w2_down[l].astype(BF16),
                 g_final, ada_row=6, final_norm=last)
    if depth == 0:
        raise ValueError("depth must be >= 1")
    return x
```

```python
import functools

import jax
import jax.numpy as jnp
import numpy as np
from jax import lax
from jax.experimental import pallas as pl
from jax.experimental.pallas import tpu as pltpu

ATT_HEADS = 8
ATT_HEAD_DIM = 128
ATT_WIDTH = ATT_HEADS * ATT_HEAD_DIM
DILATED_CONFIGS = ((128, 1), (512, 4), (2048, 16))
RET_HEADS = 8
RET_K_DIM = 64
RET_V_DIM = 128
RET_QK_WIDTH = RET_HEADS * RET_K_DIM
RET_WIDTH = RET_HEADS * RET_V_DIM
RET_CHUNK = 128
IN_WIDTH = 3 * ATT_WIDTH + 2 * RET_QK_WIDTH + 2 * RET_WIDTH
N_ADA = 9
EPS = 1e-6

V7X_LANES = 128
V7X_SUBLANES = 8
V7X_VMEM_BYTES = 64 * 1024 * 1024

MASK_VALUE = -1e30

F32 = jnp.float32
BF16 = jnp.bfloat16


def _vmem_limit(block_bytes, scratch_bytes, temp_bytes):
    need = 2 * block_bytes + scratch_bytes + temp_bytes
    assert need <= V7X_VMEM_BYTES, need
    return int(need)


def _nbytes(shape, dtype):
    return int(np.prod(shape)) * jnp.dtype(dtype).itemsize


def _rmsnorm_modulate(x, g, shift, scale):
    y = x * lax.rsqrt(jnp.mean(x * x, axis=-1, keepdims=True) + EPS) * g
    return y * (1.0 + scale) + shift


def _silu(x):
    return x * jax.nn.sigmoid(x)


def _ada_kernel(c_ref, w_ref, b_ref, o_ref):
    s = _silu(c_ref[...])
    o_ref[...] = jnp.dot(s.astype(BF16), w_ref[...].astype(BF16), preferred_element_type=F32) + b_ref[...]


def _ada(c, w_ada, b_ada):
    b, d = c.shape
    n = w_ada.shape[1]
    tn = 1024
    assert n % tn == 0
    rows = V7X_SUBLANES
    c_pad = jnp.zeros((rows, d), F32).at[:b].set(c)
    block_bytes = _nbytes((rows, d), F32) + _nbytes((d, tn), F32) + _nbytes((1, tn), F32) + _nbytes((rows, tn), F32)
    out = pl.pallas_call(
        _ada_kernel,
        out_shape=jax.ShapeDtypeStruct((rows, n), F32),
        grid=(n // tn,),
        in_specs=[
            pl.BlockSpec((rows, d), lambda j: (0, 0)),
            pl.BlockSpec((d, tn), lambda j: (0, j)),
            pl.BlockSpec((1, tn), lambda j: (0, j)),
        ],
        out_specs=pl.BlockSpec((rows, tn), lambda j: (0, j)),
        compiler_params=pltpu.CompilerParams(
            dimension_semantics=("arbitrary",),
            vmem_limit_bytes=_vmem_limit(block_bytes, 0, _nbytes((d, tn), BF16) + _nbytes((d, tn), F32)),
        ),
        name="ada",
    )(c_pad, w_ada, b_ada.reshape(1, n))
    return out[:b].reshape(b, N_ADA, d)


def _ffn_kernel(x_ref, ada_ref, g_ref, wg_ref, wu_ref, wd_ref, gfin_ref, o_ref, h_ref, *, ada_row, final_norm):
    j = pl.program_id(1)
    last = pl.num_programs(1) - 1

    def swiglu_partial(h):
        gate = jnp.dot(h, wg_ref[...], preferred_element_type=F32)
        up = jnp.dot(h, wu_ref[...], preferred_element_type=F32)
        act = (_silu(gate) * up).astype(BF16)
        return jnp.dot(act, wd_ref[...], preferred_element_type=F32)

    @pl.when(j == 0)
    def _():
        shift = ada_ref[ada_row : ada_row + 1, :]
        scale = ada_ref[ada_row + 1 : ada_row + 2, :]
        h = _rmsnorm_modulate(x_ref[...], g_ref[...], shift, scale).astype(BF16)
        h_ref[...] = h
        o_ref[...] = swiglu_partial(h)

    @pl.when((j > 0) & (j < last))
    def _():
        o_ref[...] += swiglu_partial(h_ref[...])

    @pl.when(j == last)
    def _():
        gt = ada_ref[ada_row + 2 : ada_row + 3, :]
        out = x_ref[...] + 0.5 * gt * (o_ref[...] + swiglu_partial(h_ref[...]))
        if final_norm:
            out = out * lax.rsqrt(jnp.mean(out * out, axis=-1, keepdims=True) + EPS) * gfin_ref[...]
        o_ref[...] = out


def _ffn(x, ada, g, wg, wu, wd, g_final, *, ada_row, final_norm, tm=1024, tf=512):
    b, s, d = x.shape
    dff = wg.shape[1]
    tm = min(tm, s)
    assert s % tm == 0 and dff % tf == 0 and dff // tf >= 2
    tiles_per_batch = s // tm
    x2 = x.reshape(b * s, d)
    block_bytes = (
        2 * _nbytes((tm, d), F32) + _nbytes((N_ADA, d), F32) + 2 * _nbytes((1, d), F32)
        + 2 * _nbytes((d, tf), BF16) + _nbytes((tf, d), BF16)
    )
    scratch_bytes = _nbytes((tm, d), BF16)
    temp_bytes = 3 * _nbytes((tm, tf), F32) + _nbytes((tm, d), F32)
    out = pl.pallas_call(
        functools.partial(_ffn_kernel, ada_row=ada_row, final_norm=final_norm),
        out_shape=jax.ShapeDtypeStruct((b * s, d), F32),
        grid=(b * s // tm, dff // tf),
        in_specs=[
            pl.BlockSpec((tm, d), lambda i, j: (i, 0)),
            pl.BlockSpec((None, N_ADA, d), lambda i, j: (i // tiles_per_batch, 0, 0)),
            pl.BlockSpec((1, d), lambda i, j: (0, 0)),
            pl.BlockSpec((d, tf), lambda i, j: (0, j)),
            pl.BlockSpec((d, tf), lambda i, j: (0, j)),
            pl.BlockSpec((tf, d), lambda i, j: (j, 0)),
            pl.BlockSpec((1, d), lambda i, j: (0, 0)),
        ],
        out_specs=pl.BlockSpec((tm, d), lambda i, j: (i, 0)),
        scratch_shapes=[pltpu.VMEM((tm, d), BF16)],
        compiler_params=pltpu.CompilerParams(
            dimension_semantics=("parallel", "arbitrary"),
            vmem_limit_bytes=_vmem_limit(block_bytes, scratch_bytes, temp_bytes),
        ),
        name="ffn_final" if final_norm else "ffn",
    )(x2, ada, g.reshape(1, d), wg, wu, wd, g_final.reshape(1, d))
    return out.reshape(b, s, d)


def _inproj_kernel(x_ref, ada_ref, g_ref, w_ref, o_ref, h_ref, *, ada_row):
    j = pl.program_id(1)

    @pl.when(j == 0)
    def _():
        shift = ada_ref[ada_row : ada_row + 1, :]
        scale = ada_ref[ada_row + 1 : ada_row + 2, :]
        h = _rmsnorm_modulate(x_ref[...], g_ref[...], shift, scale).astype(BF16)
        h_ref[...] = h
        o_ref[...] = jnp.dot(h, w_ref[...], preferred_element_type=F32)

    @pl.when(j > 0)
    def _():
        o_ref[...] = jnp.dot(h_ref[...], w_ref[...], preferred_element_type=F32)


def _inproj(x, ada, g, w_in, *, ada_row, tm=1024, tn=1024):
    b, s, d = x.shape
    n = w_in.shape[1]
    tm = min(tm, s)
    assert s % tm == 0 and n % tn == 0
    tiles_per_batch = s // tm
    block_bytes = (
        _nbytes((tm, d), F32) + _nbytes((N_ADA, d), F32) + _nbytes((1, d), F32)
        + _nbytes((d, tn), BF16) + _nbytes((tm, tn), F32)
    )
    out = pl.pallas_call(
        functools.partial(_inproj_kernel, ada_row=ada_row),
        out_shape=jax.ShapeDtypeStruct((b * s, n), F32),
        grid=(b * s // tm, n // tn),
        in_specs=[
            pl.BlockSpec((tm, d), lambda i, j: (i, 0)),
            pl.BlockSpec((None, N_ADA, d), lambda i, j: (i // tiles_per_batch, 0, 0)),
            pl.BlockSpec((1, d), lambda i, j: (0, 0)),
            pl.BlockSpec((d, tn), lambda i, j: (0, j)),
        ],
        out_specs=pl.BlockSpec((tm, tn), lambda i, j: (i, j)),
        scratch_shapes=[pltpu.VMEM((tm, d), BF16)],
        compiler_params=pltpu.CompilerParams(
            dimension_semantics=("parallel", "arbitrary"),
            vmem_limit_bytes=_vmem_limit(block_bytes, _nbytes((tm, d), BF16), 2 * _nbytes((tm, d), F32)),
        ),
        name="inproj",
    )(x.reshape(b * s, d), ada, g.reshape(1, d), w_in)
    return out.reshape(b, s, n)


LOG2_E = 1.4426950408889634


def _attn_kernel(q_ref, k_ref, v_ref, o_ref, xf_ref, xb_ref, ob_ref, m_ref, d_ref, bias_ref, *,
                 seq, blk, dilations, unroll):
    dh = q_ref.shape[-1]
    n_br = len(dilations)
    n_blk = seq // blk
    in_refs = (q_ref, k_ref, v_ref)
    q_scale = dh ** -0.5 * LOG2_E

    qi = lax.broadcasted_iota(jnp.int32, (blk, 2 * blk), 0)
    kj = lax.broadcasted_iota(jnp.int32, (blk, 2 * blk), 1)
    band = (kj >= qi) & (kj <= qi + blk)
    bias_ref[0] = jnp.where(band & (kj >= blk), 0.0, MASK_VALUE)
    bias_ref[1] = jnp.where(band, 0.0, MASK_VALUE)
    for li in range(n_br):
        for ti in (1, 2):
            xb_ref[li, ti, 0:blk, :] = jnp.zeros((blk, dh), BF16)

    def block_rows(t, offset=0):
        return pl.ds(pl.multiple_of(t * blk + offset, blk), blk)

    def to_mxu(x, ti):
        return (x * q_scale if ti == 0 else x).astype(BF16)

    def coarser_rows(t, li):
        r_prev, r = dilations[li - 1], dilations[li]
        f = r // r_prev
        blocks_per_seg = seq // (r * blk)
        seg = lax.div(t, blocks_per_seg)
        c = lax.rem(t, blocks_per_seg)
        start = lax.rem(seg, r_prev) * (seq // r_prev) + lax.div(seg, r_prev) + c * (blk * f)
        return pl.ds(start, blk, stride=f)

    def cast_natural(t, carry):
        for ti, ref in enumerate(in_refs):
            xb_ref[0, ti, block_rows(t, blk), :] = to_mxu(ref[block_rows(t), :], ti)
        return carry

    lax.fori_loop(0, n_blk, cast_natural, 0, unroll=unroll)

    for li in range(1, n_br):
        def gather(t, carry, li=li):
            src = coarser_rows(t, li)
            for ti, ref in enumerate(in_refs):
                x = ref[src, :] if li == 1 else xf_ref[li - 2, ti, src, :]
                if li < n_br - 1:
                    xf_ref[li - 1, ti, block_rows(t), :] = x
                xb_ref[li, ti, block_rows(t, blk), :] = to_mxu(x, ti)
            return carry

        lax.fori_loop(0, n_blk, gather, 0, unroll=unroll)

    for li in range(n_br - 1, -1, -1):
        def block(t, carry, li=li, blocks_per_seg=seq // (dilations[li] * blk)):
            n = lax.rem(t, blocks_per_seg)
            kv_rows = pl.ds(pl.multiple_of(t * blk, blk), 2 * blk)
            q = xb_ref[li, 0, block_rows(t, blk), :]
            s = lax.dot_general(q, xb_ref[li, 1, kv_rows, :], (((1,), (1,)), ((), ())), preferred_element_type=F32)
            s = s + bias_ref[jnp.minimum(n, 1)]
            m = jnp.max(s, axis=-1, keepdims=True)
            p = jnp.exp2(s - m)
            o = jnp.dot(p.astype(BF16), xb_ref[li, 2, kv_rows, :], preferred_element_type=F32)
            m = jnp.broadcast_to(m, (blk, dh))
            den = jnp.broadcast_to(jnp.sum(p, axis=-1, keepdims=True), (blk, dh))
            if li < n_br - 1:
                m_in = m_ref[li, block_rows(t), :]
                m_new = jnp.maximum(m, m_in)
                w_own, w_in = jnp.exp2(m - m_new), jnp.exp2(m_in - m_new)
                o = w_own * o + w_in * ob_ref[li, block_rows(t), :]
                den = w_own * den + w_in * d_ref[li, block_rows(t), :]
                m = m_new
            if li > 0:
                dst = coarser_rows(t, li)
                ob_ref[li - 1, dst, :] = o
                m_ref[li - 1, dst, :] = m
                d_ref[li - 1, dst, :] = den
            else:
                o_ref[block_rows(t), :] = o * (1.0 / den)
            return carry

        lax.fori_loop(0, n_blk, block, 0, unroll=n_blk)


def _attention(proj, *, q_col, k_col, v_col):
    b, s, _ = proj.shape
    dh = ATT_HEAD_DIM
    blk = DILATED_CONFIGS[0][0] // DILATED_CONFIGS[0][1]
    dilations = tuple(r for _, r in DILATED_CONFIGS)
    n_br = len(dilations)
    assert n_br >= 2 and dilations[0] == 1 and all(w // r == blk for w, r in DILATED_CONFIGS)
    assert all(dilations[i + 1] % dilations[i] == 0 for i in range(n_br - 1))
    assert all(s % (r * blk) == 0 for r in dilations)
    unroll = 4
    scratch = [
        pltpu.VMEM((max(n_br - 2, 1), 3, s, dh), F32),
        pltpu.VMEM((n_br, 3, blk + s, dh), BF16),
        pltpu.VMEM((n_br - 1, s, dh), F32),
        pltpu.VMEM((n_br - 1, s, dh), F32),
        pltpu.VMEM((n_br - 1, s, dh), F32),
        pltpu.VMEM((2, blk, 2 * blk), F32),
    ]
    block_bytes = 4 * _nbytes((s, dh), F32)
    scratch_bytes = sum(_nbytes(v.shape, v.dtype) for v in scratch)
    temp_bytes = 8 * unroll * _nbytes((blk, 2 * blk), F32)

    def spec(col):
        return pl.BlockSpec((None, s, dh), lambda bi, h, col=col: (bi, 0, col + h))

    return pl.pallas_call(
        functools.partial(_attn_kernel, seq=s, blk=blk, dilations=dilations, unroll=unroll),
        out_shape=jax.ShapeDtypeStruct((b, s, ATT_WIDTH), F32),
        grid=(b, ATT_HEADS),
        in_specs=[spec(q_col), spec(k_col), spec(v_col)],
        out_specs=pl.BlockSpec((None, s, dh), lambda bi, h: (bi, 0, h)),
        scratch_shapes=scratch,
        compiler_params=pltpu.CompilerParams(
            dimension_semantics=("parallel", "arbitrary"),
            vmem_limit_bytes=_vmem_limit(block_bytes, scratch_bytes, temp_bytes),
        ),
        name="dilated_attention",
    )(proj, proj, proj)


def _rotate_pairs(t, even_lane):
    lanes = t.shape[-1]
    return jnp.where(even_lane, pltpu.roll(t, lanes - 1, 1), pltpu.roll(t, 1, 1))


def _ret_kernel(rq_ref, rk_ref, rv_ref, rg_ref, cos_ref, sin_ref, decay_ref, zeta_ref, xi_ref, gch_ref,
                o_ref, state_ref, *, chunks):
    c_len = RET_CHUNK
    dv = RET_V_DIM
    pair_lanes = 2 * RET_K_DIM
    assert pair_lanes == V7X_LANES

    @pl.when(pl.program_id(1) == 0)
    def _():
        state_ref[...] = jnp.zeros_like(state_ref)

    lane = lax.broadcasted_iota(jnp.int32, (c_len, pair_lanes), 1)
    even_lane = (lane % 2) == 0
    first_head_lane = lane < RET_K_DIM
    first_head_row = lax.broadcasted_iota(jnp.int32, (pair_lanes, dv), 0) < RET_K_DIM
    k_scale = RET_K_DIM ** -0.5

    for c in range(chunks):
        rows = slice(c * c_len, (c + 1) * c_len)
        cosv = cos_ref[rows, :]
        sinv = sin_ref[rows, :]
        for p in range(RET_HEADS // 2):
            cols = slice(p * pair_lanes, (p + 1) * pair_lanes)
            q = rq_ref[rows, cols]
            k = rk_ref[rows, cols] * k_scale
            q = q * cosv + _rotate_pairs(q, even_lane) * sinv
            k = k * cosv + _rotate_pairs(k, even_lane) * sinv
            qx = q * xi_ref[:, cols]
            k_b = k.astype(BF16)
            state = state_ref[p]
            state_b = state.astype(BF16)
            kvs = []
            for a in range(2):
                h = 2 * p + a
                own = first_head_lane if a == 0 else jnp.logical_not(first_head_lane)
                vcols = slice(h * dv, (h + 1) * dv)
                qm = jnp.where(own, q, 0.0).astype(BF16)
                qxm = jnp.where(own, qx, 0.0).astype(BF16)
                scores = lax.dot_general(qm, k_b, (((1,), (1,)), ((), ())), preferred_element_type=F32)
                scores = scores * decay_ref[h]
                v = rv_ref[rows, vcols]
                lhs = jnp.concatenate([scores.astype(BF16), qxm], axis=1)
                rhs = jnp.concatenate([v.astype(BF16), state_b], axis=0)
                o = jnp.dot(lhs, rhs, preferred_element_type=F32)
                mu = jnp.mean(o, axis=-1, keepdims=True)
                dev = o - mu
                var = jnp.mean(dev * dev, axis=-1, keepdims=True)
                g = rg_ref[rows, vcols]
                o_ref[rows, vcols] = (dev * lax.rsqrt(var + EPS) * _silu(g)).astype(o_ref.dtype)
                vz = (v * zeta_ref[h]).astype(BF16)
                kvs.append(lax.dot_general(k_b, vz, (((0,), (0,)), ((), ())), preferred_element_type=F32))
            state_ref[p] = gch_ref[p] * state + jnp.where(first_head_row, kvs[0], kvs[1])


def _retention_tables(s):
    dk, c_len, h = RET_K_DIM, RET_CHUNK, RET_HEADS
    pos = jnp.arange(s, dtype=F32)
    angle = jnp.repeat(1.0 / (10000.0 ** jnp.linspace(0.0, 1.0, dk // 2, dtype=F32)), 2)
    theta = pos[:, None] * angle[None, :]
    sign = jnp.tile(jnp.array([-1.0, 1.0], F32), dk // 2)
    cos = jnp.tile(jnp.cos(theta), (1, 2))
    sin = jnp.tile(jnp.sin(theta) * sign[None, :], (1, 2))
    log_g = jnp.log(1.0 - 2.0 ** (-5.0 - jnp.arange(h, dtype=F32)))
    idx = jnp.arange(c_len, dtype=F32)
    diff = idx[:, None] - idx[None, :]
    decay = jnp.where(diff[None] >= 0, jnp.exp(jnp.maximum(diff, 0.0)[None] * log_g[:, None, None]), 0.0)
    zeta = jnp.exp((c_len - 1.0 - idx)[None, :] * log_g[:, None])
    zeta_b = jnp.broadcast_to(zeta[:, :, None], (h, c_len, RET_V_DIM))
    xi = jnp.exp((idx + 1.0)[:, None] * log_g[None, :])
    xi_q = jnp.repeat(xi, dk, axis=1)
    g_chunk = jnp.exp(c_len * log_g)
    gch = jnp.broadcast_to(jnp.repeat(g_chunk, dk).reshape(h // 2, 2 * dk, 1), (h // 2, 2 * dk, RET_V_DIM))
    return cos, sin, decay, zeta_b, xi_q, gch


def _retention(proj, *, q_col, k_col, v_col, g_col, tm=512):
    b, s, _ = proj.shape
    tm = min(tm, s)
    assert s % tm == 0 and tm % RET_CHUNK == 0
    qk_w, v_w = RET_QK_WIDTH, RET_WIDTH
    assert q_col % qk_w == 0 and k_col % qk_w == 0 and v_col % v_w == 0 and g_col % v_w == 0
    cos, sin, decay, zeta_b, xi_q, gch = _retention_tables(s)
    block_bytes = (
        2 * _nbytes((tm, qk_w), F32) + 2 * _nbytes((tm, v_w), F32) + 2 * _nbytes((tm, V7X_LANES), F32)
        + decay.nbytes + zeta_b.nbytes + xi_q.nbytes + gch.nbytes + _nbytes((tm, v_w), BF16)
    )
    full = lambda arr: pl.BlockSpec(arr.shape, lambda bi, t: (0,) * arr.ndim)
    return pl.pallas_call(
        functools.partial(_ret_kernel, chunks=tm // RET_CHUNK),
        out_shape=jax.ShapeDtypeStruct((b, s, v_w), BF16),
        grid=(b, s // tm),
        in_specs=[
            pl.BlockSpec((None, tm, qk_w), lambda bi, t: (bi, t, q_col // qk_w)),
            pl.BlockSpec((None, tm, qk_w), lambda bi, t: (bi, t, k_col // qk_w)),
            pl.BlockSpec((None, tm, v_w), lambda bi, t: (bi, t, v_col // v_w)),
            pl.BlockSpec((None, tm, v_w), lambda bi, t: (bi, t, g_col // v_w)),
            pl.BlockSpec((tm, V7X_LANES), lambda bi, t: (t, 0)),
            pl.BlockSpec((tm, V7X_LANES), lambda bi, t: (t, 0)),
            full(decay), full(zeta_b), full(xi_q), full(gch),
        ],
        out_specs=pl.BlockSpec((None, tm, v_w), lambda bi, t: (bi, t, 0)),
        scratch_shapes=[pltpu.VMEM((RET_HEADS // 2, 2 * RET_K_DIM, RET_V_DIM), F32)],
        compiler_params=pltpu.CompilerParams(
            dimension_semantics=("parallel", "arbitrary"),
            vmem_limit_bytes=_vmem_limit(block_bytes, gch.nbytes, 16 * _nbytes((RET_CHUNK, 2 * RET_CHUNK), F32)),
        ),
        name="retention",
    )(proj, proj, proj, proj, cos, sin, decay, zeta_b, xi_q, gch)


def _outproj_kernel(x_ref, ada_ref, att_ref, ret_ref, wa_ref, wr_ref, o_ref, *, ada_row):
    y = jnp.dot(att_ref[...].astype(BF16), wa_ref[...], preferred_element_type=F32)
    y = y + jnp.dot(ret_ref[...], wr_ref[...], preferred_element_type=F32)
    o_ref[...] = x_ref[...] + ada_ref[ada_row : ada_row + 1, :] * y


def _outproj(x, ada, att, ret, w_att, w_ret, *, ada_row, tm=512):
    b, s, d = x.shape
    tm = min(tm, s)
    assert s % tm == 0
    tiles_per_batch = s // tm
    wa, wr = att.shape[-1], ret.shape[-1]
    block_bytes = (
        2 * _nbytes((tm, d), F32) + _nbytes((N_ADA, d), F32) + _nbytes((tm, wa), att.dtype) + _nbytes((tm, wr), ret.dtype)
        + _nbytes((wa, d), BF16) + _nbytes((wr, d), BF16)
    )
    out = pl.pallas_call(
        functools.partial(_outproj_kernel, ada_row=ada_row),
        out_shape=jax.ShapeDtypeStruct((b * s, d), F32),
        grid=(b * s // tm,),
        in_specs=[
            pl.BlockSpec((tm, d), lambda i: (i, 0)),
            pl.BlockSpec((None, N_ADA, d), lambda i: (i // tiles_per_batch, 0, 0)),
            pl.BlockSpec((tm, wa), lambda i: (i, 0)),
            pl.BlockSpec((tm, wr), lambda i: (i, 0)),
            pl.BlockSpec((wa, d), lambda i: (0, 0)),
            pl.BlockSpec((wr, d), lambda i: (0, 0)),
        ],
        out_specs=pl.BlockSpec((tm, d), lambda i: (i, 0)),
        compiler_params=pltpu.CompilerParams(
            dimension_semantics=("parallel",),
            vmem_limit_bytes=_vmem_limit(block_bytes, 0, 2 * _nbytes((tm, d), F32)),
        ),
        name="outproj",
    )(x.reshape(b * s, d), ada, att.reshape(b * s, wa), ret.reshape(b * s, wr), w_att, w_ret)
    return out.reshape(b, s, d)


def kernel(x, c, w_ada, b_ada, g_ffn1, w1_gate, w1_up, w1_down, g_mix, w_in, w_out, g_ffn2, w2_gate, w2_up, w2_down, g_final):
    depth = w_ada.shape[0]
    att_w = ATT_WIDTH
    for l in range(depth):
        last = l == depth - 1
        ada = _ada(c, w_ada[l], b_ada[l])
        x = _ffn(x, ada, g_ffn1[l], w1_gate[l].astype(BF16), w1_up[l].astype(BF16), w1_down[l].astype(BF16),
                 g_final, ada_row=0, final_norm=False)
        proj = _inproj(x, ada, g_mix[l], w_in[l].astype(BF16), ada_row=3)
        lane_blocks = att_w // V7X_LANES
        att = _attention(proj, q_col=0, k_col=lane_blocks, v_col=2 * lane_blocks)
        ret = _retention(proj, q_col=3 * att_w, k_col=3 * att_w + RET_QK_WIDTH,
                         v_col=3 * att_w + 2 * RET_QK_WIDTH, g_col=3 * att_w + 2 * RET_QK_WIDTH + RET_WIDTH)
        w_o = w_out[l].astype(BF16)
        x = _outproj(x, ada, att, ret, w_o[:att_w], w_o[att_w:], ada_row=5)
        x = _ffn(x, ada, g_ffn2[l], w2_gate[l].astype(BF16), w2_up[l].astype(BF16), w2_down[l].astype(BF16),
                 g_final, ada_row=6, final_norm=last)
    if depth == 0:
        raise ValueError("depth must be >= 1")
    return x
```

```python
import functools

import jax
import jax.numpy as jnp
import numpy as np
from jax import lax
from jax.experimental import pallas as pl
from jax.experimental.pallas import tpu as pltpu

ATT_HEADS = 8
ATT_HEAD_DIM = 128
ATT_WIDTH = ATT_HEADS * ATT_HEAD_DIM
DILATED_CONFIGS = ((128, 1), (512, 4), (2048, 16))
RET_HEADS = 8
RET_K_DIM = 64
RET_V_DIM = 128
RET_QK_WIDTH = RET_HEADS * RET_K_DIM
RET_WIDTH = RET_HEADS * RET_V_DIM
RET_CHUNK = 128
IN_WIDTH = 3 * ATT_WIDTH + 2 * RET_QK_WIDTH + 2 * RET_WIDTH
N_ADA = 9
EPS = 1e-6

V7X_LANES = 128
V7X_SUBLANES = 8
V7X_VMEM_BYTES = 64 * 1024 * 1024

MASK_VALUE = -1e30

F32 = jnp.float32
BF16 = jnp.bfloat16


def _vmem_limit(block_bytes, scratch_bytes, temp_bytes):
    need = 2 * block_bytes + scratch_bytes + temp_bytes
    assert need <= V7X_VMEM_BYTES, need
    return int(need)


def _nbytes(shape, dtype):
    return int(np.prod(shape)) * jnp.dtype(dtype).itemsize


def _rmsnorm_modulate(x, g, shift, scale):
    y = x * lax.rsqrt(jnp.mean(x * x, axis=-1, keepdims=True) + EPS) * g
    return y * (1.0 + scale) + shift


def _silu(x):
    return x * jax.nn.sigmoid(x)


def _ada_kernel(c_ref, w_ref, b_ref, o_ref):
    s = _silu(c_ref[...])
    o_ref[...] = jnp.dot(s.astype(BF16), w_ref[...].astype(BF16), preferred_element_type=F32) + b_ref[...]


def _ada(c, w_ada, b_ada):
    b, d = c.shape
    n = w_ada.shape[1]
    tn = 1024
    assert n % tn == 0
    rows = V7X_SUBLANES
    c_pad = jnp.zeros((rows, d), F32).at[:b].set(c)
    block_bytes = _nbytes((rows, d), F32) + _nbytes((d, tn), F32) + _nbytes((1, tn), F32) + _nbytes((rows, tn), F32)
    out = pl.pallas_call(
        _ada_kernel,
        out_shape=jax.ShapeDtypeStruct((rows, n), F32),
        grid=(n // tn,),
        in_specs=[
            pl.BlockSpec((rows, d), lambda j: (0, 0)),
            pl.BlockSpec((d, tn), lambda j: (0, j)),
            pl.BlockSpec((1, tn), lambda j: (0, j)),
        ],
        out_specs=pl.BlockSpec((rows, tn), lambda j: (0, j)),
        compiler_params=pltpu.CompilerParams(
            dimension_semantics=("arbitrary",),
            vmem_limit_bytes=_vmem_limit(block_bytes, 0, _nbytes((d, tn), BF16) + _nbytes((d, tn), F32)),
        ),
        name="ada",
    )(c_pad, w_ada, b_ada.reshape(1, n))
    return out[:b].reshape(b, N_ADA, d)


def _ffn_kernel(*refs, ada_row, final_norm, n_side):
    x_ref, ada_ref, g_ref, wg_ref, wu_ref, wd_ref, gfin_ref = refs[:7]
    side_src = refs[7 : 7 + n_side]
    o_ref = refs[7 + n_side]
    side_dst = refs[8 + n_side : 8 + 2 * n_side]
    h_ref = refs[8 + 2 * n_side]
    j = pl.program_id(1)
    last = pl.num_programs(1) - 1

    def swiglu_partial(h):
        for src, dst in zip(side_src, side_dst):
            dst[...] = src[...].astype(BF16)
        gate = jnp.dot(h, wg_ref[...], preferred_element_type=F32)
        up = jnp.dot(h, wu_ref[...], preferred_element_type=F32)
        act = (_silu(gate) * up).astype(BF16)
        return jnp.dot(act, wd_ref[...], preferred_element_type=F32)

    @pl.when(j == 0)
    def _():
        shift = ada_ref[ada_row : ada_row + 1, :]
        scale = ada_ref[ada_row + 1 : ada_row + 2, :]
        h = _rmsnorm_modulate(x_ref[...], g_ref[...], shift, scale).astype(BF16)
        h_ref[...] = h
        o_ref[...] = swiglu_partial(h)

    @pl.when((j > 0) & (j < last))
    def _():
        o_ref[...] += swiglu_partial(h_ref[...])

    @pl.when(j == last)
    def _():
        gt = ada_ref[ada_row + 2 : ada_row + 3, :]
        out = x_ref[...] + 0.5 * gt * (o_ref[...] + swiglu_partial(h_ref[...]))
        if final_norm:
            out = out * lax.rsqrt(jnp.mean(out * out, axis=-1, keepdims=True) + EPS) * gfin_ref[...]
        o_ref[...] = out


def _side_cast_spec(shape, n_i, n_j):
    rows, cols = shape
    bf16_rows = 2 * V7X_SUBLANES
    options = [
        (br * bc, br, bc)
        for br in range(bf16_rows, rows + 1, bf16_rows) if rows % br == 0
        for bc in range(V7X_LANES, cols + 1, V7X_LANES) if cols % bc == 0
        if (rows // br) * (cols // bc) <= n_i * n_j
    ]
    _, br, bc = min(options)
    n_cb = cols // bc
    n_blocks = (rows // br) * n_cb

    def index_map(i, j):
        blk = jnp.minimum(i * n_j + j, n_blocks - 1)
        return (blk // n_cb, blk % n_cb)

    return pl.BlockSpec((br, bc), index_map)


def _ffn(x, ada, g, wg, wu, wd, g_final, *, ada_row, final_norm, side_weights=(), tm=1024, tf=512):
    b, s, d = x.shape
    dff = wg.shape[1]
    tm = min(tm, s)
    assert s % tm == 0 and dff % tf == 0 and dff // tf >= 2
    tiles_per_batch = s // tm
    n_i, n_j = b * s // tm, dff // tf
    x2 = x.reshape(b * s, d)
    side_specs = [_side_cast_spec(w.shape, n_i, n_j) for w in side_weights]
    side_bytes = sum(_nbytes(sp.block_shape, F32) + _nbytes(sp.block_shape, BF16) for sp in side_specs)
    block_bytes = (
        2 * _nbytes((tm, d), F32) + _nbytes((N_ADA, d), F32) + 2 * _nbytes((1, d), F32)
        + 2 * _nbytes((d, tf), BF16) + _nbytes((tf, d), BF16) + side_bytes
    )
    scratch_bytes = _nbytes((tm, d), BF16)
    temp_bytes = 3 * _nbytes((tm, tf), F32) + _nbytes((tm, d), BF16)
    outs = pl.pallas_call(
        functools.partial(_ffn_kernel, ada_row=ada_row, final_norm=final_norm, n_side=len(side_weights)),
        out_shape=[jax.ShapeDtypeStruct((b * s, d), F32)] + [jax.ShapeDtypeStruct(w.shape, BF16) for w in side_weights],
        grid=(n_i, n_j),
        in_specs=[
            pl.BlockSpec((tm, d), lambda i, j: (i, 0)),
            pl.BlockSpec((None, N_ADA, d), lambda i, j: (i // tiles_per_batch, 0, 0)),
            pl.BlockSpec((1, d), lambda i, j: (0, 0)),
            pl.BlockSpec((d, tf), lambda i, j: (0, j)),
            pl.BlockSpec((d, tf), lambda i, j: (0, j)),
            pl.BlockSpec((tf, d), lambda i, j: (j, 0)),
            pl.BlockSpec((1, d), lambda i, j: (0, 0)),
        ] + side_specs,
        out_specs=[pl.BlockSpec((tm, d), lambda i, j: (i, 0))] + side_specs,
        scratch_shapes=[pltpu.VMEM((tm, d), BF16)],
        compiler_params=pltpu.CompilerParams(
            dimension_semantics=("parallel", "arbitrary"),
            vmem_limit_bytes=_vmem_limit(block_bytes, scratch_bytes, temp_bytes),
        ),
        name="ffn_final" if final_norm else "ffn",
    )(x2, ada, g.reshape(1, d), wg, wu, wd, g_final.reshape(1, d), *side_weights)
    return (outs[0].reshape(b, s, d), *outs[1:])


def _inproj_kernel(x_ref, ada_ref, g_ref, w_ref, o_ref, h_ref, *, ada_row):
    j = pl.program_id(1)

    @pl.when(j == 0)
    def _():
        shift = ada_ref[ada_row : ada_row + 1, :]
        scale = ada_ref[ada_row + 1 : ada_row + 2, :]
        h = _rmsnorm_modulate(x_ref[...], g_ref[...], shift, scale).astype(BF16)
        h_ref[...] = h
        o_ref[...] = jnp.dot(h, w_ref[...], preferred_element_type=F32)

    @pl.when(j > 0)
    def _():
        o_ref[...] = jnp.dot(h_ref[...], w_ref[...], preferred_element_type=F32)


def _inproj(x, ada, g, w_in, *, ada_row, tm=1024, tn=1024):
    b, s, d = x.shape
    n = w_in.shape[1]
    tm = min(tm, s)
    assert s % tm == 0 and n % tn == 0
    tiles_per_batch = s // tm
    block_bytes = (
        _nbytes((tm, d), F32) + _nbytes((N_ADA, d), F32) + _nbytes((1, d), F32)
        + _nbytes((d, tn), BF16) + _nbytes((tm, tn), F32)
    )
    out = pl.pallas_call(
        functools.partial(_inproj_kernel, ada_row=ada_row),
        out_shape=jax.ShapeDtypeStruct((b * s, n), F32),
        grid=(b * s // tm, n // tn),
        in_specs=[
            pl.BlockSpec((tm, d), lambda i, j: (i, 0)),
            pl.BlockSpec((None, N_ADA, d), lambda i, j: (i // tiles_per_batch, 0, 0)),
            pl.BlockSpec((1, d), lambda i, j: (0, 0)),
            pl.BlockSpec((d, tn), lambda i, j: (0, j)),
        ],
        out_specs=pl.BlockSpec((tm, tn), lambda i, j: (i, j)),
        scratch_shapes=[pltpu.VMEM((tm, d), BF16)],
        compiler_params=pltpu.CompilerParams(
            dimension_semantics=("parallel", "arbitrary"),
            vmem_limit_bytes=_vmem_limit(block_bytes, _nbytes((tm, d), BF16), 2 * _nbytes((tm, d), F32)),
        ),
        name="inproj",
    )(x.reshape(b * s, d), ada, g.reshape(1, d), w_in)
    return out.reshape(b, s, n)


LOG2_E = 1.4426950408889634


def _attn_kernel(q_ref, k_ref, v_ref, o_ref, xf_ref, xb_ref, ob_ref, m_ref, d_ref, bias_ref, *,
                 seq, blk, dilations, unroll):
    dh = q_ref.shape[-1]
    n_br = len(dilations)
    n_blk = seq // blk
    in_refs = (q_ref, k_ref, v_ref)
    q_scale = dh ** -0.5 * LOG2_E

    qi = lax.broadcasted_iota(jnp.int32, (blk, 2 * blk), 0)
    kj = lax.broadcasted_iota(jnp.int32, (blk, 2 * blk), 1)
    band = (kj >= qi) & (kj <= qi + blk)
    bias_ref[0] = jnp.where(band & (kj >= blk), 0.0, MASK_VALUE)
    bias_ref[1] = jnp.where(band, 0.0, MASK_VALUE)
    for li in range(n_br):
        for ti in (1, 2):
            xb_ref[li, ti, 0:blk, :] = jnp.zeros((blk, dh), BF16)

    def block_rows(t, offset=0):
        return pl.ds(pl.multiple_of(t * blk + offset, blk), blk)

    def to_mxu(x, ti):
        return (x * q_scale if ti == 0 else x).astype(BF16)

    def coarser_rows(t, li):
        r_prev, r = dilations[li - 1], dilations[li]
        f = r // r_prev
        blocks_per_seg = seq // (r * blk)
        seg = lax.div(t, blocks_per_seg)
        c = lax.rem(t, blocks_per_seg)
        start = lax.rem(seg, r_prev) * (seq // r_prev) + lax.div(seg, r_prev) + c * (blk * f)
        return pl.ds(start, blk, stride=f)

    def cast_natural(t, carry):
        for ti, ref in enumerate(in_refs):
            xb_ref[0, ti, block_rows(t, blk), :] = to_mxu(ref[block_rows(t), :], ti)
        return carry

    lax.fori_loop(0, n_blk, cast_natural, 0, unroll=unroll)

    for li in range(1, n_br):
        def gather(t, carry, li=li):
            src = coarser_rows(t, li)
            for ti, ref in enumerate(in_refs):
                x = ref[src, :] if li == 1 else xf_ref[li - 2, ti, src, :]
                if li < n_br - 1:
                    xf_ref[li - 1, ti, block_rows(t), :] = x
                xb_ref[li, ti, block_rows(t, blk), :] = to_mxu(x, ti)
            return carry

        lax.fori_loop(0, n_blk, gather, 0, unroll=unroll)

    for li in range(n_br - 1, -1, -1):
        def block(t, carry, li=li, blocks_per_seg=seq // (dilations[li] * blk)):
            n = lax.rem(t, blocks_per_seg)
            kv_rows = pl.ds(pl.multiple_of(t * blk, blk), 2 * blk)
            q = xb_ref[li, 0, block_rows(t, blk), :]
            s = lax.dot_general(q, xb_ref[li, 1, kv_rows, :], (((1,), (1,)), ((), ())), preferred_element_type=F32)
            s = s + bias_ref[jnp.minimum(n, 1)]
            m = jnp.max(s, axis=-1, keepdims=True)
            p = jnp.exp2(s - m)
            o = jnp.dot(p.astype(BF16), xb_ref[li, 2, kv_rows, :], preferred_element_type=F32)
            m = jnp.broadcast_to(m, (blk, dh))
            den = jnp.broadcast_to(jnp.sum(p, axis=-1, keepdims=True), (blk, dh))
            if li < n_br - 1:
                m_in = m_ref[li, block_rows(t), :]
                m_new = jnp.maximum(m, m_in)
                w_own, w_in = jnp.exp2(m - m_new), jnp.exp2(m_in - m_new)
                o = w_own * o + w_in * ob_ref[li, block_rows(t), :]
                den = w_own * den + w_in * d_ref[li, block_rows(t), :]
                m = m_new
            if li > 0:
                dst = coarser_rows(t, li)
                ob_ref[li - 1, dst, :] = o
                m_ref[li - 1, dst, :] = m
                d_ref[li - 1, dst, :] = den
            else:
                o_ref[block_rows(t), :] = o * (1.0 / den)
            return carry

        lax.fori_loop(0, n_blk, block, 0, unroll=n_blk)


def _attention(proj, *, q_col, k_col, v_col):
    b, s, _ = proj.shape
    dh = ATT_HEAD_DIM
    blk = DILATED_CONFIGS[0][0] // DILATED_CONFIGS[0][1]
    dilations = tuple(r for _, r in DILATED_CONFIGS)
    n_br = len(dilations)
    assert n_br >= 2 and dilations[0] == 1 and all(w // r == blk for w, r in DILATED_CONFIGS)
    assert all(dilations[i + 1] % dilations[i] == 0 for i in range(n_br - 1))
    assert all(s % (r * blk) == 0 for r in dilations)
    unroll = 4
    scratch = [
        pltpu.VMEM((max(n_br - 2, 1), 3, s, dh), F32),
        pltpu.VMEM((n_br, 3, blk + s, dh), BF16),
        pltpu.VMEM((n_br - 1, s, dh), F32),
        pltpu.VMEM((n_br - 1, s, dh), F32),
        pltpu.VMEM((n_br - 1, s, dh), F32),
        pltpu.VMEM((2, blk, 2 * blk), F32),
    ]
    block_bytes = 4 * _nbytes((s, dh), F32)
    scratch_bytes = sum(_nbytes(v.shape, v.dtype) for v in scratch)
    temp_bytes = 8 * unroll * _nbytes((blk, 2 * blk), F32)

    def spec(col):
        return pl.BlockSpec((None, s, dh), lambda bi, h, col=col: (bi, 0, col + h))

    return pl.pallas_call(
        functools.partial(_attn_kernel, seq=s, blk=blk, dilations=dilations, unroll=unroll),
        out_shape=jax.ShapeDtypeStruct((b, s, ATT_WIDTH), F32),
        grid=(b, ATT_HEADS),
        in_specs=[spec(q_col), spec(k_col), spec(v_col)],
        out_specs=pl.BlockSpec((None, s, dh), lambda bi, h: (bi, 0, h)),
        scratch_shapes=scratch,
        compiler_params=pltpu.CompilerParams(
            dimension_semantics=("parallel", "arbitrary"),
            vmem_limit_bytes=_vmem_limit(block_bytes, scratch_bytes, temp_bytes),
        ),
        name="dilated_attention",
    )(proj, proj, proj)


def _rotate_pairs(t, even_lane):
    lanes = t.shape[-1]
    return jnp.where(even_lane, pltpu.roll(t, lanes - 1, 1), pltpu.roll(t, 1, 1))


def _ret_kernel(rq_ref, rk_ref, rv_ref, rg_ref, cos_ref, sin_ref, decay_ref, zeta_ref, xi_ref, gch_ref,
                o_ref, state_ref, *, chunks):
    c_len = RET_CHUNK
    dv = RET_V_DIM
    pair_lanes = 2 * RET_K_DIM
    assert pair_lanes == V7X_LANES

    @pl.when(pl.program_id(1) == 0)
    def _():
        state_ref[...] = jnp.zeros_like(state_ref)

    lane = lax.broadcasted_iota(jnp.int32, (c_len, pair_lanes), 1)
    even_lane = (lane % 2) == 0
    first_head_lane = lane < RET_K_DIM
    first_head_row = lax.broadcasted_iota(jnp.int32, (pair_lanes, dv), 0) < RET_K_DIM
    k_scale = RET_K_DIM ** -0.5

    for c in range(chunks):
        rows = slice(c * c_len, (c + 1) * c_len)
        cosv = cos_ref[rows, :]
        sinv = sin_ref[rows, :]
        for p in range(RET_HEADS // 2):
            cols = slice(p * pair_lanes, (p + 1) * pair_lanes)
            q = rq_ref[rows, cols]
            k = rk_ref[rows, cols] * k_scale
            q = q * cosv + _rotate_pairs(q, even_lane) * sinv
            k = k * cosv + _rotate_pairs(k, even_lane) * sinv
            qx = q * xi_ref[:, cols]
            k_b = k.astype(BF16)
            state = state_ref[p]
            state_b = state.astype(BF16)
            kvs = []
            for a in range(2):
                h = 2 * p + a
                own = first_head_lane if a == 0 else jnp.logical_not(first_head_lane)
                vcols = slice(h * dv, (h + 1) * dv)
                qm = jnp.where(own, q, 0.0).astype(BF16)
                qxm = jnp.where(own, qx, 0.0).astype(BF16)
                scores = lax.dot_general(qm, k_b, (((1,), (1,)), ((), ())), preferred_element_type=F32)
                scores = scores * decay_ref[h]
                v = rv_ref[rows, vcols]
                lhs = jnp.concatenate([scores.astype(BF16), qxm], axis=1)
                rhs = jnp.concatenate([v.astype(BF16), state_b], axis=0)
                o = jnp.dot(lhs, rhs, preferred_element_type=F32)
                mu = jnp.mean(o, axis=-1, keepdims=True)
                dev = o - mu
                var = jnp.mean(dev * dev, axis=-1, keepdims=True)
                g = rg_ref[rows, vcols]
                o_ref[rows, vcols] = (dev * lax.rsqrt(var + EPS) * _silu(g)).astype(o_ref.dtype)
                vz = (v * zeta_ref[h]).astype(BF16)
                kvs.append(lax.dot_general(k_b, vz, (((0,), (0,)), ((), ())), preferred_element_type=F32))
            state_ref[p] = gch_ref[p] * state + jnp.where(first_head_row, kvs[0], kvs[1])


def _retention_tables(s):
    dk, c_len, h = RET_K_DIM, RET_CHUNK, RET_HEADS
    pos = jnp.arange(s, dtype=F32)
    angle = jnp.repeat(1.0 / (10000.0 ** jnp.linspace(0.0, 1.0, dk // 2, dtype=F32)), 2)
    theta = pos[:, None] * angle[None, :]
    sign = jnp.tile(jnp.array([-1.0, 1.0], F32), dk // 2)
    cos = jnp.tile(jnp.cos(theta), (1, 2))
    sin = jnp.tile(jnp.sin(theta) * sign[None, :], (1, 2))
    log_g = jnp.log(1.0 - 2.0 ** (-5.0 - jnp.arange(h, dtype=F32)))
    idx = jnp.arange(c_len, dtype=F32)
    diff = idx[:, None] - idx[None, :]
    decay = jnp.where(diff[None] >= 0, jnp.exp(jnp.maximum(diff, 0.0)[None] * log_g[:, None, None]), 0.0)
    zeta = jnp.exp((c_len - 1.0 - idx)[None, :] * log_g[:, None])
    zeta_b = jnp.broadcast_to(zeta[:, :, None], (h, c_len, RET_V_DIM))
    xi = jnp.exp((idx + 1.0)[:, None] * log_g[None, :])
    xi_q = jnp.repeat(xi, dk, axis=1)
    g_chunk = jnp.exp(c_len * log_g)
    gch = jnp.broadcast_to(jnp.repeat(g_chunk, dk).reshape(h // 2, 2 * dk, 1), (h // 2, 2 * dk, RET_V_DIM))
    return cos, sin, decay, zeta_b, xi_q, gch


def _retention(proj, *, q_col, k_col, v_col, g_col, tm=512):
    b, s, _ = proj.shape
    tm = min(tm, s)
    assert s % tm == 0 and tm % RET_CHUNK == 0
    qk_w, v_w = RET_QK_WIDTH, RET_WIDTH
    assert q_col % qk_w == 0 and k_col % qk_w == 0 and v_col % v_w == 0 and g_col % v_w == 0
    cos, sin, decay, zeta_b, xi_q, gch = _retention_tables(s)
    block_bytes = (
        2 * _nbytes((tm, qk_w), F32) + 2 * _nbytes((tm, v_w), F32) + 2 * _nbytes((tm, V7X_LANES), F32)
        + decay.nbytes + zeta_b.nbytes + xi_q.nbytes + gch.nbytes + _nbytes((tm, v_w), BF16)
    )
    full = lambda arr: pl.BlockSpec(arr.shape, lambda bi, t: (0,) * arr.ndim)
    return pl.pallas_call(
        functools.partial(_ret_kernel, chunks=tm // RET_CHUNK),
        out_shape=jax.ShapeDtypeStruct((b, s, v_w), BF16),
        grid=(b, s // tm),
        in_specs=[
            pl.BlockSpec((None, tm, qk_w), lambda bi, t: (bi, t, q_col // qk_w)),
            pl.BlockSpec((None, tm, qk_w), lambda bi, t: (bi, t, k_col // qk_w)),
            pl.BlockSpec((None, tm, v_w), lambda bi, t: (bi, t, v_col // v_w)),
            pl.BlockSpec((None, tm, v_w), lambda bi, t: (bi, t, g_col // v_w)),
            pl.BlockSpec((tm, V7X_LANES), lambda bi, t: (t, 0)),
            pl.BlockSpec((tm, V7X_LANES), lambda bi, t: (t, 0)),
            full(decay), full(zeta_b), full(xi_q), full(gch),
        ],
        out_specs=pl.BlockSpec((None, tm, v_w), lambda bi, t: (bi, t, 0)),
        scratch_shapes=[pltpu.VMEM((RET_HEADS // 2, 2 * RET_K_DIM, RET_V_DIM), F32)],
        compiler_params=pltpu.CompilerParams(
            dimension_semantics=("parallel", "arbitrary"),
            vmem_limit_bytes=_vmem_limit(block_bytes, gch.nbytes, 16 * _nbytes((RET_CHUNK, 2 * RET_CHUNK), F32)),
        ),
        name="retention",
    )(proj, proj, proj, proj, cos, sin, decay, zeta_b, xi_q, gch)


def _outproj_kernel(x_ref, ada_ref, att_ref, ret_ref, wa_ref, wr_ref, o_ref, *, ada_row):
    y = jnp.dot(att_ref[...].astype(BF16), wa_ref[...], preferred_element_type=F32)
    y = y + jnp.dot(ret_ref[...], wr_ref[...], preferred_element_type=F32)
    o_ref[...] = x_ref[...] + ada_ref[ada_row : ada_row + 1, :] * y


def _outproj(x, ada, att, ret, w_out, *, ada_row, tm=512):
    b, s, d = x.shape
    tm = min(tm, s)
    assert s % tm == 0
    tiles_per_batch = s // tm
    wa, wr = att.shape[-1], ret.shape[-1]
    assert wa == wr and w_out.shape == (wa + wr, d)
    block_bytes = (
        2 * _nbytes((tm, d), F32) + _nbytes((N_ADA, d), F32) + _nbytes((tm, wa), att.dtype) + _nbytes((tm, wr), ret.dtype)
        + _nbytes((wa, d), BF16) + _nbytes((wr, d), BF16)
    )
    out = pl.pallas_call(
        functools.partial(_outproj_kernel, ada_row=ada_row),
        out_shape=jax.ShapeDtypeStruct((b * s, d), F32),
        grid=(b * s // tm,),
        in_specs=[
            pl.BlockSpec((tm, d), lambda i: (i, 0)),
            pl.BlockSpec((None, N_ADA, d), lambda i: (i // tiles_per_batch, 0, 0)),
            pl.BlockSpec((tm, wa), lambda i: (i, 0)),
            pl.BlockSpec((tm, wr), lambda i: (i, 0)),
            pl.BlockSpec((wa, d), lambda i: (0, 0)),
            pl.BlockSpec((wr, d), lambda i: (1, 0)),
        ],
        out_specs=pl.BlockSpec((tm, d), lambda i: (i, 0)),
        compiler_params=pltpu.CompilerParams(
            dimension_semantics=("parallel",),
            vmem_limit_bytes=_vmem_limit(block_bytes, 0, 2 * _nbytes((tm, d), F32)),
        ),
        name="outproj",
    )(x.reshape(b * s, d), ada, att.reshape(b * s, wa), ret.reshape(b * s, wr), w_out, w_out)
    return out.reshape(b, s, d)


def kernel(x, c, w_ada, b_ada, g_ffn1, w1_gate, w1_up, w1_down, g_mix, w_in, w_out, g_ffn2, w2_gate, w2_up, w2_down, g_final):
    depth = w_ada.shape[0]
    att_w = ATT_WIDTH
    for l in range(depth):
        last = l == depth - 1
        ada = _ada(c, w_ada[l], b_ada[l])
        x, w_in_b, w_o, w2g, w2u, w2d = _ffn(
            x, ada, g_ffn1[l], w1_gate[l].astype(BF16), w1_up[l].astype(BF16), w1_down[l].astype(BF16), g_final,
            ada_row=0, final_norm=False, side_weights=(w_in[l], w_out[l], w2_gate[l], w2_up[l], w2_down[l]))
        proj = _inproj(x, ada, g_mix[l], w_in_b, ada_row=3)
        lane_blocks = att_w // V7X_LANES
        att = _attention(proj, q_col=0, k_col=lane_blocks, v_col=2 * lane_blocks)
        ret = _retention(proj, q_col=3 * att_w, k_col=3 * att_w + RET_QK_WIDTH,
                         v_col=3 * att_w + 2 * RET_QK_WIDTH, g_col=3 * att_w + 2 * RET_QK_WIDTH + RET_WIDTH)
        x = _outproj(x, ada, att, ret, w_o, ada_row=5)
        (x,) = _ffn(x, ada, g_ffn2[l], w2g, w2u, w2d, g_final, ada_row=6, final_norm=last)
    if depth == 0:
        raise ValueError("depth must be >= 1")
    return x
```

```python
import functools

import jax
import jax.numpy as jnp
import numpy as np
from jax import lax
from jax.experimental import pallas as pl
from jax.experimental.pallas import tpu as pltpu

ATT_HEADS = 8
ATT_HEAD_DIM = 128
ATT_WIDTH = ATT_HEADS * ATT_HEAD_DIM
DILATED_CONFIGS = ((128, 1), (512, 4), (2048, 16))
RET_HEADS = 8
RET_K_DIM = 64
RET_V_DIM = 128
RET_QK_WIDTH = RET_HEADS * RET_K_DIM
RET_WIDTH = RET_HEADS * RET_V_DIM
RET_CHUNK = 128
N_QKV = 3
IN_WIDTH = N_QKV * ATT_WIDTH + 2 * RET_QK_WIDTH + 2 * RET_WIDTH
N_ADA = 9
EPS = 1e-6

V7X_LANES = 128
V7X_SUBLANES = 8
V7X_VMEM_BYTES = 64 * 1024 * 1024

MASK_VALUE = -1e30

F32 = jnp.float32
BF16 = jnp.bfloat16


def _vmem_limit(block_bytes, scratch_bytes, temp_bytes):
    need = 2 * block_bytes + scratch_bytes + temp_bytes
    assert need <= V7X_VMEM_BYTES, need
    return int(need)


def _nbytes(shape, dtype):
    return int(np.prod(shape)) * jnp.dtype(dtype).itemsize


def _rmsnorm_modulate(x, g, shift, scale):
    y = x * lax.rsqrt(jnp.mean(x * x, axis=-1, keepdims=True) + EPS) * g
    return y * (1.0 + scale) + shift


def _silu(x):
    return x * jax.nn.sigmoid(x)


def _ada_kernel(c_ref, w_ref, b_ref, o_ref):
    s = _silu(c_ref[...])
    o_ref[...] = jnp.dot(s.astype(BF16), w_ref[...].astype(BF16), preferred_element_type=F32) + b_ref[...]


def _ada(c, w_ada, b_ada):
    b, d = c.shape
    n = w_ada.shape[1]
    tn = 1024
    assert n % tn == 0
    rows = V7X_SUBLANES
    c_pad = jnp.zeros((rows, d), F32).at[:b].set(c)
    block_bytes = _nbytes((rows, d), F32) + _nbytes((d, tn), F32) + _nbytes((1, tn), F32) + _nbytes((rows, tn), F32)
    out = pl.pallas_call(
        _ada_kernel,
        out_shape=jax.ShapeDtypeStruct((rows, n), F32),
        grid=(n // tn,),
        in_specs=[
            pl.BlockSpec((rows, d), lambda j: (0, 0)),
            pl.BlockSpec((d, tn), lambda j: (0, j)),
            pl.BlockSpec((1, tn), lambda j: (0, j)),
        ],
        out_specs=pl.BlockSpec((rows, tn), lambda j: (0, j)),
        compiler_params=pltpu.CompilerParams(
            dimension_semantics=("arbitrary",),
            vmem_limit_bytes=_vmem_limit(block_bytes, 0, _nbytes((d, tn), BF16) + _nbytes((d, tn), F32)),
        ),
        name="ada",
    )(c_pad, w_ada, b_ada.reshape(1, n))
    return out[:b].reshape(b, N_ADA, d)


def _ffn_kernel(*refs, ada_row, final_norm, n_side):
    x_ref, ada_ref, g_ref, wg_ref, wu_ref, wd_ref, gfin_ref = refs[:7]
    side_src = refs[7 : 7 + n_side]
    o_ref = refs[7 + n_side]
    side_dst = refs[8 + n_side : 8 + 2 * n_side]
    h_ref = refs[8 + 2 * n_side]
    j = pl.program_id(1)
    last = pl.num_programs(1) - 1

    def swiglu_partial(h):
        for src, dst in zip(side_src, side_dst):
            dst[...] = src[...].astype(BF16)
        gate = jnp.dot(h, wg_ref[...], preferred_element_type=F32)
        up = jnp.dot(h, wu_ref[...], preferred_element_type=F32)
        act = (_silu(gate) * up).astype(BF16)
        return jnp.dot(act, wd_ref[...], preferred_element_type=F32)

    @pl.when(j == 0)
    def _():
        shift = ada_ref[ada_row : ada_row + 1, :]
        scale = ada_ref[ada_row + 1 : ada_row + 2, :]
        h = _rmsnorm_modulate(x_ref[...], g_ref[...], shift, scale).astype(BF16)
        h_ref[...] = h
        o_ref[...] = swiglu_partial(h)

    @pl.when((j > 0) & (j < last))
    def _():
        o_ref[...] += swiglu_partial(h_ref[...])

    @pl.when(j == last)
    def _():
        gt = ada_ref[ada_row + 2 : ada_row + 3, :]
        out = x_ref[...] + 0.5 * gt * (o_ref[...] + swiglu_partial(h_ref[...]))
        if final_norm:
            out = out * lax.rsqrt(jnp.mean(out * out, axis=-1, keepdims=True) + EPS) * gfin_ref[...]
        o_ref[...] = out


def _side_cast_spec(shape, n_i, n_j):
    rows, cols = shape
    bf16_rows = 2 * V7X_SUBLANES
    options = [
        (br * bc, br, bc)
        for br in range(bf16_rows, rows + 1, bf16_rows) if rows % br == 0
        for bc in range(V7X_LANES, cols + 1, V7X_LANES) if cols % bc == 0
        if (rows // br) * (cols // bc) <= n_i * n_j
    ]
    _, br, bc = min(options)
    n_cb = cols // bc
    n_blocks = (rows // br) * n_cb

    def index_map(i, j):
        blk = jnp.minimum(i * n_j + j, n_blocks - 1)
        return (blk // n_cb, blk % n_cb)

    return pl.BlockSpec((br, bc), index_map)


def _ffn(x, ada, g, wg, wu, wd, g_final, *, ada_row, final_norm, side_weights=(), tm=1024, tf=512):
    b, s, d = x.shape
    dff = wg.shape[1]
    tm = min(tm, s)
    assert s % tm == 0 and dff % tf == 0 and dff // tf >= 2
    tiles_per_batch = s // tm
    n_i, n_j = b * s // tm, dff // tf
    x2 = x.reshape(b * s, d)
    side_specs = [_side_cast_spec(w.shape, n_i, n_j) for w in side_weights]
    side_bytes = sum(_nbytes(sp.block_shape, F32) + _nbytes(sp.block_shape, BF16) for sp in side_specs)
    block_bytes = (
        2 * _nbytes((tm, d), F32) + _nbytes((N_ADA, d), F32) + 2 * _nbytes((1, d), F32)
        + 2 * _nbytes((d, tf), BF16) + _nbytes((tf, d), BF16) + side_bytes
    )
    scratch_bytes = _nbytes((tm, d), BF16)
    temp_bytes = 3 * _nbytes((tm, tf), F32) + _nbytes((tm, d), BF16)
    outs = pl.pallas_call(
        functools.partial(_ffn_kernel, ada_row=ada_row, final_norm=final_norm, n_side=len(side_weights)),
        out_shape=[jax.ShapeDtypeStruct((b * s, d), F32)] + [jax.ShapeDtypeStruct(w.shape, BF16) for w in side_weights],
        grid=(n_i, n_j),
        in_specs=[
            pl.BlockSpec((tm, d), lambda i, j: (i, 0)),
            pl.BlockSpec((None, N_ADA, d), lambda i, j: (i // tiles_per_batch, 0, 0)),
            pl.BlockSpec((1, d), lambda i, j: (0, 0)),
            pl.BlockSpec((d, tf), lambda i, j: (0, j)),
            pl.BlockSpec((d, tf), lambda i, j: (0, j)),
            pl.BlockSpec((tf, d), lambda i, j: (j, 0)),
            pl.BlockSpec((1, d), lambda i, j: (0, 0)),
        ] + side_specs,
        out_specs=[pl.BlockSpec((tm, d), lambda i, j: (i, 0))] + side_specs,
        scratch_shapes=[pltpu.VMEM((tm, d), BF16)],
        compiler_params=pltpu.CompilerParams(
            dimension_semantics=("parallel", "arbitrary"),
            vmem_limit_bytes=_vmem_limit(block_bytes, scratch_bytes, temp_bytes),
        ),
        name="ffn_final" if final_norm else "ffn",
    )(x2, ada, g.reshape(1, d), wg, wu, wd, g_final.reshape(1, d), *side_weights)
    return (outs[0].reshape(b, s, d), *outs[1:])


def _inproj_kernel(x_ref, ada_ref, g_ref, w_ref, *refs, ada_row, dilations, q_scale, n_tiles):
    n_lay = len(dilations)
    lay_refs = refs[:n_lay]
    ret_ref, h_ref, sc_ref = refs[n_lay : n_lay + 3]
    j = pl.program_id(1)
    tm = x_ref.shape[0]
    n_slabs = w_ref.shape[1] // V7X_LANES

    def emit_layouts(res, scale):
        for s in range(n_slabs):
            x = res[:, s * V7X_LANES : (s + 1) * V7X_LANES]
            if scale != 1.0:
                x = x * scale
            lay_refs[0][s] = x.astype(BF16)
            src = 0
            for li in range(1, n_lay):
                r_prev, r = dilations[li - 1], dilations[li]
                f = r // r_prev
                if li == 1:
                    sc_ref[src, s] = x
                rows = tm // r
                for seg in range(r):
                    start = (seg % r_prev) * (tm // r_prev) + seg // r_prev
                    g = sc_ref[src, s, pl.ds(start, rows, stride=f), :]
                    lay_refs[li][s, seg] = g.astype(BF16)
                    if li < n_lay - 1:
                        sc_ref[1 - src, s, seg * rows : (seg + 1) * rows, :] = g
                src = 1 - src

    def project(h, jj):
        res = jnp.dot(h, w_ref[...], preferred_element_type=F32)
        if jj < N_QKV:
            emit_layouts(res, q_scale if jj == 0 else 1.0)
        else:
            ret_ref[...] = res.astype(ret_ref.dtype)

    @pl.when(j == 0)
    def _():
        shift = ada_ref[ada_row : ada_row + 1, :]
        scale = ada_ref[ada_row + 1 : ada_row + 2, :]
        h = _rmsnorm_modulate(x_ref[...], g_ref[...], shift, scale).astype(BF16)
        h_ref[...] = h
        project(h, 0)

    for jj in range(1, n_tiles):
        @pl.when(j == jj)
        def _(jj=jj):
            project(h_ref[...], jj)


LOG2_E = 1.4426950408889634


def _inproj(x, ada, g, w_in, *, ada_row, tm=1024):
    b, s, d = x.shape
    n = w_in.shape[1]
    tn = ATT_WIDTH
    heads, dh = ATT_HEADS, ATT_HEAD_DIM
    dilations = tuple(r for _, r in DILATED_CONFIGS)
    tm = min(tm, s)
    assert s % tm == 0 and n % tn == 0 and all(tm % (r * 2 * V7X_SUBLANES) == 0 for r in dilations)
    assert dilations[0] == 1 and all(dilations[i + 1] % dilations[i] == 0 for i in range(len(dilations) - 1))
    tiles_per_batch = s // tm
    n_tiles = n // tn
    n_ret = n - N_QKV * tn

    def lay_shape(r, rows):
        return (heads, rows, dh) if r == 1 else (heads, r, rows // r, dh)

    def lay_spec(r):
        nd = len(lay_shape(r, tm))
        return pl.BlockSpec(
            (None, None) + lay_shape(r, tm),
            lambda i, j: (i // tiles_per_batch, jnp.minimum(j, N_QKV - 1)) + (0,) * (nd - 2) + (i % tiles_per_batch, 0))

    lay_bytes = len(dilations) * _nbytes((heads, tm, dh), BF16)
    scratch = [pltpu.VMEM((tm, d), BF16), pltpu.VMEM((2, heads, tm, dh), F32)]
    block_bytes = (
        _nbytes((tm, d), F32) + _nbytes((N_ADA, d), F32) + _nbytes((1, d), F32)
        + _nbytes((d, tn), BF16) + lay_bytes + _nbytes((tm, tn), BF16)
    )
    outs = pl.pallas_call(
        functools.partial(_inproj_kernel, ada_row=ada_row, dilations=dilations,
                          q_scale=dh ** -0.5 * LOG2_E, n_tiles=n_tiles),
        out_shape=[jax.ShapeDtypeStruct((b, N_QKV) + lay_shape(r, s), BF16) for r in dilations]
        + [jax.ShapeDtypeStruct((b * s, n_ret), BF16)],
        grid=(b * s // tm, n_tiles),
        in_specs=[
            pl.BlockSpec((tm, d), lambda i, j: (i, 0)),
            pl.BlockSpec((None, N_ADA, d), lambda i, j: (i // tiles_per_batch, 0, 0)),
            pl.BlockSpec((1, d), lambda i, j: (0, 0)),
            pl.BlockSpec((d, tn), lambda i, j: (0, j)),
        ],
        out_specs=[lay_spec(r) for r in dilations]
        + [pl.BlockSpec((tm, tn), lambda i, j: (i, jnp.maximum(j - N_QKV, 0)))],
        scratch_shapes=scratch,
        compiler_params=pltpu.CompilerParams(
            dimension_semantics=("parallel", "arbitrary"),
            vmem_limit_bytes=_vmem_limit(block_bytes, sum(_nbytes(v.shape, v.dtype) for v in scratch),
                                         _nbytes((tm, tn), F32) + _nbytes((tm, tn), BF16)),
        ),
        name="inproj",
    )(x.reshape(b * s, d), ada, g.reshape(1, d), w_in)
    lays = [o.reshape(b, N_QKV, heads, s, dh) for o in outs[:-1]]
    return lays, outs[-1].reshape(b, s, n_ret)


def _attn_kernel(*refs, seq, blk, dilations):
    n_br = len(dilations)
    lay = [refs[3 * li : 3 * li + 3] for li in range(n_br)]
    o_ref, ob_ref, m_ref, d_ref, bias_ref = refs[3 * n_br :]
    dh = o_ref.shape[-1]
    n_blk = seq // blk

    qi = lax.broadcasted_iota(jnp.int32, (blk, 2 * blk), 0)
    kj = lax.broadcasted_iota(jnp.int32, (blk, 2 * blk), 1)
    band = (kj >= qi) & (kj <= qi + blk)
    bias_ref[0] = jnp.where(band, 0.0, MASK_VALUE)
    bias_ref[1] = jnp.where(band & (kj >= blk), 0.0, MASK_VALUE)
    bias_ref[2] = jnp.where(kj <= qi, 0.0, MASK_VALUE)

    def rows(t):
        return slice(t * blk, (t + 1) * blk)

    def coarser_rows(t, li):
        r_prev, r = dilations[li - 1], dilations[li]
        f = r // r_prev
        seg, c = divmod(t, seq // (r * blk))
        start = (seg % r_prev) * (seq // r_prev) + seg // r_prev + c * (blk * f)
        return pl.ds(start, blk, stride=f)

    for li in range(n_br - 1, -1, -1):
        q_ref, k_ref, v_ref = lay[li]
        blocks_per_seg = seq // (dilations[li] * blk)
        for t in range(n_blk):
            kv_rows = slice(max(t - 1, 0) * blk, (max(t - 1, 0) + 2) * blk)
            bias = bias_ref[2 if t == 0 else (1 if t % blocks_per_seg == 0 else 0)]
            s = lax.dot_general(q_ref[rows(t), :], k_ref[kv_rows, :], (((1,), (1,)), ((), ())),
                                preferred_element_type=F32) + bias
            m = jnp.max(s, axis=-1, keepdims=True)
            p = jnp.exp2(s - m)
            o = jnp.dot(p.astype(BF16), v_ref[kv_rows, :], preferred_element_type=F32)
            m = jnp.broadcast_to(m, (blk, dh))
            den = jnp.broadcast_to(jnp.sum(p, axis=-1, keepdims=True), (blk, dh))
            if li < n_br - 1:
                m_in = m_ref[li, rows(t), :]
                m_new = jnp.maximum(m, m_in)
                w_own, w_in = jnp.exp2(m - m_new), jnp.exp2(m_in - m_new)
                o = w_own * o + w_in * ob_ref[li, rows(t), :]
                den = w_own * den + w_in * d_ref[li, rows(t), :]
                m = m_new
            if li > 0:
                dst = coarser_rows(t, li)
                ob_ref[li - 1, dst, :] = o
                m_ref[li - 1, dst, :] = m
                d_ref[li - 1, dst, :] = den
            else:
                o_ref[rows(t), :] = o * (1.0 / den)


def _attention(lays):
    b, _, heads, s, dh = lays[0].shape
    blk = DILATED_CONFIGS[0][0] // DILATED_CONFIGS[0][1]
    dilations = tuple(r for _, r in DILATED_CONFIGS)
    n_br = len(dilations)
    assert n_br >= 2 and dilations[0] == 1 and all(w // r == blk for w, r in DILATED_CONFIGS)
    assert all(dilations[i + 1] % dilations[i] == 0 for i in range(n_br - 1))
    assert all(s % (r * blk) == 0 for r in dilations) and s >= 2 * blk
    scratch = [
        pltpu.VMEM((n_br - 1, s, dh), F32),
        pltpu.VMEM((n_br - 1, s, dh), F32),
        pltpu.VMEM((n_br - 1, s, dh), F32),
        pltpu.VMEM((3, blk, 2 * blk), F32),
    ]
    block_bytes = N_QKV * n_br * _nbytes((s, dh), BF16) + _nbytes((s, dh), F32)
    scratch_bytes = sum(_nbytes(v.shape, v.dtype) for v in scratch)
    temp_bytes = 64 * _nbytes((blk, 2 * blk), F32)

    def spec(ti):
        return pl.BlockSpec((None, None, None, s, dh), lambda bi, h, ti=ti: (bi, ti, h, 0, 0))

    return pl.pallas_call(
        functools.partial(_attn_kernel, seq=s, blk=blk, dilations=dilations),
        out_shape=jax.ShapeDtypeStruct((b, s, heads * dh), F32),
        grid=(b, heads),
        in_specs=[spec(ti) for _ in range(n_br) for ti in range(N_QKV)],
        out_specs=pl.BlockSpec((None, s, dh), lambda bi, h: (bi, 0, h)),
        scratch_shapes=scratch,
        compiler_params=pltpu.CompilerParams(
            dimension_semantics=("parallel", "arbitrary"),
            vmem_limit_bytes=_vmem_limit(block_bytes, scratch_bytes, temp_bytes),
        ),
        name="dilated_attention",
    )(*[lay for lay in lays for _ in range(N_QKV)])


def _rotate_pairs(t, even_lane):
    lanes = t.shape[-1]
    return jnp.where(even_lane, pltpu.roll(t, lanes - 1, 1), pltpu.roll(t, 1, 1))


def _ret_kernel(rq_ref, rk_ref, rv_ref, rg_ref, cos_ref, sin_ref, decay_ref, zeta_ref, xi_ref, gch_ref,
                o_ref, state_ref, kb_ref, qm_ref, qx_ref, vz_ref, rhs_ref, *, chunks, out_unroll):
    c_len = RET_CHUNK
    dv = RET_V_DIM
    pair_lanes = 2 * RET_K_DIM
    assert pair_lanes == V7X_LANES

    @pl.when(pl.program_id(1) == 0)
    def _():
        state_ref[...] = jnp.zeros_like(state_ref)

    lane = lax.broadcasted_iota(jnp.int32, (c_len, pair_lanes), 1)
    even_lane = (lane % 2) == 0
    first_head_lane = lane < RET_K_DIM
    first_head_row = lax.broadcasted_iota(jnp.int32, (pair_lanes, dv), 0) < RET_K_DIM
    k_scale = RET_K_DIM ** -0.5

    n_pairs = RET_HEADS // 2
    nt_dims = (((1,), (1,)), ((), ()))
    tn_dims = (((0,), (0,)), ((), ()))

    def chunk_rows(c):
        return slice(c * c_len, (c + 1) * c_len)

    for c in range(chunks):
        rows = chunk_rows(c)
        for p in range(n_pairs):
            cols = slice(p * pair_lanes, (p + 1) * pair_lanes)
            q = rq_ref[rows, cols].astype(F32)
            k = rk_ref[rows, cols].astype(F32) * k_scale
            q = q * cos_ref[rows, :] + _rotate_pairs(q, even_lane) * sin_ref[rows, :]
            k = k * cos_ref[rows, :] + _rotate_pairs(k, even_lane) * sin_ref[rows, :]
            qx = q * xi_ref[:, cols]
            kb_ref[c, p] = k.astype(BF16)
            for a in range(2):
                h = 2 * p + a
                own = first_head_lane if a == 0 else jnp.logical_not(first_head_lane)
                v = rv_ref[rows, h * dv : (h + 1) * dv]
                qx_ref[c, h] = jnp.where(own, qx, 0.0).astype(BF16)
                qm_ref[c, h] = jnp.where(own, q, 0.0).astype(BF16)
                rhs_ref[c, h, :c_len, :] = v
                v = v.astype(F32)
                vz_ref[c, h] = (v * zeta_ref[h]).astype(BF16)

    for p in range(n_pairs):
        state = state_ref[p]
        for c in range(chunks):
            kvs = [lax.dot_general(kb_ref[c, p], vz_ref[c, 2 * p + a], tn_dims, preferred_element_type=F32)
                   for a in range(2)]
            state_b = state.astype(BF16)
            for a in range(2):
                rhs_ref[c, 2 * p + a, c_len:, :] = state_b
            state = gch_ref[p] * state + jnp.where(first_head_row, kvs[0], kvs[1])
        state_ref[p] = state

    def outputs(c, carry):
        rows = pl.ds(pl.multiple_of(c * c_len, c_len), c_len)
        scores = [lax.dot_general(qm_ref[c, h], kb_ref[c, h // 2], nt_dims, preferred_element_type=F32)
                  for h in range(RET_HEADS)]
        for h in range(RET_HEADS):
            vcols = slice(h * dv, (h + 1) * dv)
            lhs = jnp.concatenate([(scores[h] * decay_ref[h]).astype(BF16), qx_ref[c, h]], axis=1)
            o = jnp.dot(lhs, rhs_ref[c, h], preferred_element_type=F32)
            mu = jnp.mean(o, axis=-1, keepdims=True)
            dev = o - mu
            var = jnp.mean(dev * dev, axis=-1, keepdims=True)
            g = rg_ref[rows, vcols].astype(F32)
            o_ref[rows, vcols] = (dev * lax.rsqrt(var + EPS) * _silu(g)).astype(o_ref.dtype)
        return carry

    lax.fori_loop(0, chunks, outputs, 0, unroll=out_unroll)


def _retention_tables(s):
    dk, c_len, h = RET_K_DIM, RET_CHUNK, RET_HEADS
    pos = jnp.arange(s, dtype=F32)
    angle = jnp.repeat(1.0 / (10000.0 ** jnp.linspace(0.0, 1.0, dk // 2, dtype=F32)), 2)
    theta = pos[:, None] * angle[None, :]
    sign = jnp.tile(jnp.array([-1.0, 1.0], F32), dk // 2)
    cos = jnp.tile(jnp.cos(theta), (1, 2))
    sin = jnp.tile(jnp.sin(theta) * sign[None, :], (1, 2))
    log_g = jnp.log(1.0 - 2.0 ** (-5.0 - jnp.arange(h, dtype=F32)))
    idx = jnp.arange(c_len, dtype=F32)
    diff = idx[:, None] - idx[None, :]
    decay = jnp.where(diff[None] >= 0, jnp.exp(jnp.maximum(diff, 0.0)[None] * log_g[:, None, None]), 0.0)
    zeta = jnp.exp((c_len - 1.0 - idx)[None, :] * log_g[:, None])
    zeta_b = jnp.broadcast_to(zeta[:, :, None], (h, c_len, RET_V_DIM))
    xi = jnp.exp((idx + 1.0)[:, None] * log_g[None, :])
    xi_q = jnp.repeat(xi, dk, axis=1)
    g_chunk = jnp.exp(c_len * log_g)
    gch = jnp.broadcast_to(jnp.repeat(g_chunk, dk).reshape(h // 2, 2 * dk, 1), (h // 2, 2 * dk, RET_V_DIM))
    return cos, sin, decay, zeta_b, xi_q, gch


def _retention(proj, *, q_col, k_col, v_col, g_col, tm=1024):
    b, s, _ = proj.shape
    tm = min(tm, s)
    assert s % tm == 0 and tm % RET_CHUNK == 0
    qk_w, v_w = RET_QK_WIDTH, RET_WIDTH
    assert q_col % qk_w == 0 and k_col % qk_w == 0 and v_col % v_w == 0 and g_col % v_w == 0
    cos, sin, decay, zeta_b, xi_q, gch = _retention_tables(s)
    block_bytes = (
        2 * _nbytes((tm, qk_w), proj.dtype) + 2 * _nbytes((tm, v_w), proj.dtype) + 2 * _nbytes((tm, V7X_LANES), F32)
        + decay.nbytes + zeta_b.nbytes + xi_q.nbytes + gch.nbytes + _nbytes((tm, v_w), BF16)
    )
    full = lambda arr: pl.BlockSpec(arr.shape, lambda bi, t: (0,) * arr.ndim)
    chunks, c_len, pair = tm // RET_CHUNK, RET_CHUNK, 2 * RET_K_DIM
    scratch = [
        pltpu.VMEM((RET_HEADS // 2, pair, RET_V_DIM), F32),
        pltpu.VMEM((chunks, RET_HEADS // 2, c_len, pair), BF16),
        pltpu.VMEM((chunks, RET_HEADS, c_len, pair), BF16),
        pltpu.VMEM((chunks, RET_HEADS, c_len, pair), BF16),
        pltpu.VMEM((chunks, RET_HEADS, c_len, RET_V_DIM), BF16),
        pltpu.VMEM((chunks, RET_HEADS, c_len + pair, RET_V_DIM), BF16),
    ]
    out_unroll = 4
    assert chunks % out_unroll == 0 and chunks > out_unroll
    scratch_bytes = sum(_nbytes(v.shape, v.dtype) for v in scratch)
    return pl.pallas_call(
        functools.partial(_ret_kernel, chunks=chunks, out_unroll=out_unroll),
        out_shape=jax.ShapeDtypeStruct((b, s, v_w), BF16),
        grid=(b, s // tm),
        in_specs=[
            pl.BlockSpec((None, tm, qk_w), lambda bi, t: (bi, t, q_col // qk_w)),
            pl.BlockSpec((None, tm, qk_w), lambda bi, t: (bi, t, k_col // qk_w)),
            pl.BlockSpec((None, tm, v_w), lambda bi, t: (bi, t, v_col // v_w)),
            pl.BlockSpec((None, tm, v_w), lambda bi, t: (bi, t, g_col // v_w)),
            pl.BlockSpec((tm, V7X_LANES), lambda bi, t: (t, 0)),
            pl.BlockSpec((tm, V7X_LANES), lambda bi, t: (t, 0)),
            full(decay), full(zeta_b), full(xi_q), full(gch),
        ],
        out_specs=pl.BlockSpec((None, tm, v_w), lambda bi, t: (bi, t, 0)),
        scratch_shapes=scratch,
        compiler_params=pltpu.CompilerParams(
            dimension_semantics=("parallel", "arbitrary"),
            vmem_limit_bytes=_vmem_limit(block_bytes, scratch_bytes, 16 * _nbytes((RET_CHUNK, 2 * RET_CHUNK), F32)),
        ),
        name="retention",
    )(proj, proj, proj, proj, cos, sin, decay, zeta_b, xi_q, gch)


def _outproj_kernel(x_ref, ada_ref, att_ref, ret_ref, wa_ref, wr_ref, o_ref, *, ada_row):
    y = jnp.dot(att_ref[...].astype(BF16), wa_ref[...], preferred_element_type=F32)
    y = y + jnp.dot(ret_ref[...], wr_ref[...], preferred_element_type=F32)
    o_ref[...] = x_ref[...] + ada_ref[ada_row : ada_row + 1, :] * y


def _outproj(x, ada, att, ret, w_out, *, ada_row, tm=512):
    b, s, d = x.shape
    tm = min(tm, s)
    assert s % tm == 0
    tiles_per_batch = s // tm
    wa, wr = att.shape[-1], ret.shape[-1]
    assert wa == wr and w_out.shape == (wa + wr, d)
    block_bytes = (
        2 * _nbytes((tm, d), F32) + _nbytes((N_ADA, d), F32) + _nbytes((tm, wa), att.dtype) + _nbytes((tm, wr), ret.dtype)
        + _nbytes((wa, d), BF16) + _nbytes((wr, d), BF16)
    )
    out = pl.pallas_call(
        functools.partial(_outproj_kernel, ada_row=ada_row),
        out_shape=jax.ShapeDtypeStruct((b * s, d), F32),
        grid=(b * s // tm,),
        in_specs=[
            pl.BlockSpec((tm, d), lambda i: (i, 0)),
            pl.BlockSpec((None, N_ADA, d), lambda i: (i // tiles_per_batch, 0, 0)),
            pl.BlockSpec((tm, wa), lambda i: (i, 0)),
            pl.BlockSpec((tm, wr), lambda i: (i, 0)),
            pl.BlockSpec((wa, d), lambda i: (0, 0)),
            pl.BlockSpec((wr, d), lambda i: (1, 0)),
        ],
        out_specs=pl.BlockSpec((tm, d), lambda i: (i, 0)),
        compiler_params=pltpu.CompilerParams(
            dimension_semantics=("parallel",),
            vmem_limit_bytes=_vmem_limit(block_bytes, 0, 2 * _nbytes((tm, d), F32)),
        ),
        name="outproj",
    )(x.reshape(b * s, d), ada, att.reshape(b * s, wa), ret.reshape(b * s, wr), w_out, w_out)
    return out.reshape(b, s, d)


def kernel(x, c, w_ada, b_ada, g_ffn1, w1_gate, w1_up, w1_down, g_mix, w_in, w_out, g_ffn2, w2_gate, w2_up, w2_down, g_final):
    depth = w_ada.shape[0]
    att_w = ATT_WIDTH
    for l in range(depth):
        last = l == depth - 1
        ada = _ada(c, w_ada[l], b_ada[l])
        x, w_in_b, w_o, w2g, w2u, w2d = _ffn(
            x, ada, g_ffn1[l], w1_gate[l].astype(BF16), w1_up[l].astype(BF16), w1_down[l].astype(BF16), g_final,
            ada_row=0, final_norm=False, side_weights=(w_in[l], w_out[l], w2_gate[l], w2_up[l], w2_down[l]))
        lays, ret_in = _inproj(x, ada, g_mix[l], w_in_b, ada_row=3)
        att = _attention(lays)
        ret = _retention(ret_in, q_col=0, k_col=RET_QK_WIDTH, v_col=2 * RET_QK_WIDTH,
                         g_col=2 * RET_QK_WIDTH + RET_WIDTH)
        x = _outproj(x, ada, att, ret, w_o, ada_row=5)
        (x,) = _ffn(x, ada, g_ffn2[l], w2g, w2u, w2d, g_final, ada_row=6, final_norm=last)
    if depth == 0:
        raise ValueError("depth must be >= 1")
    return x
```
